```python
import jax, jax.numpy as jnp
from jax import lax
import numpy as np

D_MODEL = 1024
BATCH = 8
SEQ = 8192
DEPTH = 1
DEC_BATCH = 128
DEC_SEQ = 8
PAST_LEN = 8192
PAGE_SIZE = 128

POOL_WIDTH = D_MODEL // 2
POOL_WINDOWS = (2, 4, 8, 16)
N_POOL_GROUPS = len(POOL_WINDOWS)
POOL_GROUP = POOL_WIDTH // N_POOL_GROUPS
POOL_STATE = max(POOL_WINDOWS) - 1
N_HEADS = 8
HEAD_DIM = 64
ATTN_WIDTH = N_HEADS * HEAD_DIM
Q_BLOCK = 128
SB_BIAS_INIT = -7.0
D_FF = ((8 * D_MODEL // 3 + 127) // 128) * 128
CONV_WIDTH = 3
CONV_STATE = CONV_WIDTH - 1
IN_WIDTH = POOL_WIDTH + 3 * ATTN_WIDTH + 2 * D_MODEL
EPS = 1e-6
PAGE_POOL_NUM, PAGE_POOL_DEN = 5, 4

kernel_name = 'hybrid_pool_stickbreak_convffn_step'


def _rmsnorm(x, g):
    xf = x.astype(jnp.float32)
    y = xf * lax.rsqrt(jnp.mean(xf * xf, axis=-1, keepdims=True) + EPS)
    return y.astype(x.dtype) * g


def _pool_mixer(u, prev, start_pos, pool_w, pool_scale):
    B, T, _ = u.shape
    ext = jnp.concatenate([prev.astype(u.dtype), u], axis=1)
    csum = jnp.cumsum(ext.astype(jnp.float32), axis=1)
    csum = jnp.concatenate([jnp.zeros_like(csum[:, :1]), csum], axis=1)
    end = csum[:, POOL_STATE + 1:]
    cur = ext[:, POOL_STATE:].astype(jnp.float32)
    pos = start_pos + jnp.arange(T, dtype=jnp.int32)
    groups = []
    for g, w in enumerate(POOL_WINDOWS):
        c0, c1 = g * POOL_GROUP, (g + 1) * POOL_GROUP
        begin = csum[:, POOL_STATE + 1 - w: POOL_STATE + 1 - w + T, c0:c1]
        cnt = jnp.minimum(pos + 1, w).astype(jnp.float32)[None, :, None]
        groups.append((end[..., c0:c1] - begin) / cnt - cur[..., c0:c1])
    pooled = jnp.stack(groups, axis=2).astype(u.dtype)
    mixed = jnp.einsum('btgc,gcd->btgd', pooled, pool_w).reshape(B, T, POOL_WIDTH) * pool_scale
    return mixed, ext[:, -POOL_STATE:]


def _sb_block(q_blk, k, v, bias, q_pos, k_pos):
    z = (jnp.einsum('bhqd,bhkd->bhqk', q_blk, k).astype(jnp.float32) * (HEAD_DIM ** -0.5)
         + bias.astype(jnp.float32)[None, :, None, None])
    mask = k_pos[None, :] < q_pos[:, None]
    log_1mb = jnp.where(mask, jax.nn.log_sigmoid(-z), 0.0)
    right = lax.cumsum(log_1mb, axis=3, reverse=True) - log_1mb
    w = jnp.where(mask, jnp.exp(jax.nn.log_sigmoid(z) + right), 0.0)
    return jnp.einsum('bhqk,bhkd->bhqd', w.astype(v.dtype), v)


def _stick_breaking(q, k, v, bias, q_pos, k_pos):
    B, T, H, d = q.shape
    blk = min(Q_BLOCK, T)
    nb = T // blk
    kt = jnp.transpose(k, (0, 2, 1, 3))
    vt = jnp.transpose(v, (0, 2, 1, 3))
    qb = jnp.transpose(q.reshape(B, nb, blk, H, d), (1, 0, 3, 2, 4))
    pb = q_pos.reshape(nb, blk)
    out = lax.map(lambda a: _sb_block(a[0], kt, vt, bias, a[1], k_pos), (qb, pb))
    return jnp.transpose(out, (1, 0, 3, 2, 4)).reshape(B, T, H * d)


def _conv_ffn(h, prev, w_up_a, w_up_b, conv_w, conv_b, w_down):
    T = h.shape[1]
    a = h @ w_up_a
    gate = h @ w_up_b
    ext = jnp.concatenate([prev.astype(a.dtype), a], axis=1)
    conv = conv_b + ext[:, 0:T] * conv_w[0]
    for j in range(1, CONV_WIDTH):
        conv = conv + ext[:, j:j + T] * conv_w[j]
    hid = jax.nn.gelu(conv) * gate
    return hid @ w_down, ext[:, -CONV_STATE:]


def _layer(x, start_pos, pool_prev, conv_prev, k_past, v_past,
           norm_mix, w_in, sb_bias, pool_w, pool_scale, w_proj_a, w_proj_b, w_out,
           norm_ffn, w_up_a, w_up_b, conv_w, conv_b, w_down):
    B, T, _ = x.shape
    h = _rmsnorm(x, norm_mix)
    proj = h @ w_in
    o1 = POOL_WIDTH
    o2 = o1 + ATTN_WIDTH
    o3 = o2 + ATTN_WIDTH
    o4 = o3 + ATTN_WIDTH
    o5 = o4 + D_MODEL
    u = proj[..., :o1]
    q = proj[..., o1:o2].reshape(B, T, N_HEADS, HEAD_DIM)
    k = proj[..., o2:o3].reshape(B, T, N_HEADS, HEAD_DIM)
    v = proj[..., o3:o4].reshape(B, T, N_HEADS, HEAD_DIM)
    gate_a = jax.nn.sigmoid(proj[..., o4:o5])
    gate_b = jax.nn.sigmoid(proj[..., o5:])
    y_a, pool_state = _pool_mixer(u, pool_prev, start_pos, pool_w, pool_scale)
    k_all = jnp.concatenate([k_past.astype(k.dtype), k], axis=1)
    v_all = jnp.concatenate([v_past.astype(v.dtype), v], axis=1)
    q_pos = start_pos + jnp.arange(T, dtype=jnp.int32)
    k_pos = jnp.arange(start_pos + T, dtype=jnp.int32)
    y_b = _stick_breaking(q, k_all, v_all, sb_bias, q_pos, k_pos)
    x = x + (gate_a * (y_a @ w_proj_a) + gate_b * (y_b @ w_proj_b)) @ w_out
    f, conv_state = _conv_ffn(_rmsnorm(x, norm_ffn), conv_prev, w_up_a, w_up_b, conv_w, conv_b, w_down)
    return x + f, k, v, pool_state, conv_state


def setup_inputs(seed: int = 0) -> dict:
    key = jax.random.key(seed)
    ks = jax.random.split(key, 24)
    n_pages = PAST_LEN // PAGE_SIZE
    n_used = DEC_BATCH * n_pages
    n_phys = (n_used * PAGE_POOL_NUM) // PAGE_POOL_DEN
    f32 = jnp.float32
    nrm = lambda k, shape, s: jax.random.normal(k, shape, f32) * s
    page_table = jax.random.permutation(ks[0], n_phys)[:n_used].reshape(DEC_BATCH, n_pages).astype(jnp.int32)
    return {
        'x_prompt': nrm(ks[1], (BATCH, SEQ, D_MODEL), 1.0),
        'x_sample': nrm(ks[2], (DEC_BATCH, DEC_SEQ, D_MODEL), 1.0),
        'cache_k': nrm(ks[3], (DEPTH, n_phys, PAGE_SIZE, N_HEADS, HEAD_DIM), 1.0),
        'cache_v': nrm(ks[4], (DEPTH, n_phys, PAGE_SIZE, N_HEADS, HEAD_DIM), 1.0),
        'page_table': page_table,
        'state_pool': nrm(ks[5], (DEPTH, DEC_BATCH, POOL_STATE, POOL_WIDTH), 1.0),
        'state_conv': nrm(ks[6], (DEPTH, DEC_BATCH, CONV_STATE, D_FF), 1.0),
        'norm_mix': 1.0 + nrm(ks[7], (DEPTH, D_MODEL), 0.1),
        'w_in': nrm(ks[8], (DEPTH, D_MODEL, IN_WIDTH), D_MODEL ** -0.5),
        'sb_bias': SB_BIAS_INIT + nrm(ks[21], (DEPTH, N_HEADS), 0.1),
        'pool_w': nrm(ks[9], (DEPTH, N_POOL_GROUPS, POOL_GROUP, POOL_GROUP), POOL_GROUP ** -0.5),
        'pool_scale': 1.0 + nrm(ks[10], (DEPTH, POOL_WIDTH), 0.1),
        'w_proj_a': nrm(ks[11], (DEPTH, POOL_WIDTH, D_MODEL), POOL_WIDTH ** -0.5),
        'w_proj_b': nrm(ks[12], (DEPTH, ATTN_WIDTH, D_MODEL), ATTN_WIDTH ** -0.5),
        'w_out': nrm(ks[13], (DEPTH, D_MODEL, D_MODEL), D_MODEL ** -0.5),
        'norm_ffn': 1.0 + nrm(ks[14], (DEPTH, D_MODEL), 0.1),
        'w_up_a': nrm(ks[15], (DEPTH, D_MODEL, D_FF), D_MODEL ** -0.5),
        'w_up_b': nrm(ks[16], (DEPTH, D_MODEL, D_FF), D_MODEL ** -0.5),
        'conv_w': nrm(ks[17], (DEPTH, CONV_WIDTH, D_FF), CONV_WIDTH ** -0.5),
        'conv_b': nrm(ks[18], (DEPTH, D_FF), 0.01),
        'w_down': nrm(ks[19], (DEPTH, D_FF, D_MODEL), D_FF ** -0.5),
        'norm_final': 1.0 + nrm(ks[20], (D_MODEL,), 0.1),
    }


def reference(x_prompt, x_sample, cache_k, cache_v, page_table, state_pool, state_conv,
              norm_mix, w_in, sb_bias, pool_w, pool_scale, w_proj_a, w_proj_b, w_out,
              norm_ffn, w_up_a, w_up_b, conv_w, conv_b, w_down, norm_final):
    bp = x_prompt.shape[0]
    bs = x_sample.shape[0]
    past_len = page_table.shape[1] * PAGE_SIZE
    hp, hs = x_prompt, x_sample
    kp_l, vp_l, pp_l, cp_l = [], [], [], []
    ks_l, vs_l, ps_l, cs_l = [], [], [], []
    for l in range(DEPTH):
        w = (norm_mix[l], w_in[l], sb_bias[l], pool_w[l], pool_scale[l], w_proj_a[l], w_proj_b[l], w_out[l],
             norm_ffn[l], w_up_a[l], w_up_b[l], conv_w[l], conv_b[l], w_down[l])
        empty_kv = jnp.zeros((bp, 0, N_HEADS, HEAD_DIM), x_prompt.dtype)
        pool0 = jnp.zeros((bp, POOL_STATE, POOL_WIDTH), x_prompt.dtype)
        conv0 = jnp.zeros((bp, CONV_STATE, D_FF), x_prompt.dtype)
        hp, kp, vp, pp, cp = _layer(hp, 0, pool0, conv0, empty_kv, empty_kv, *w)
        k_past = cache_k[l][page_table].reshape(bs, past_len, N_HEADS, HEAD_DIM)
        v_past = cache_v[l][page_table].reshape(bs, past_len, N_HEADS, HEAD_DIM)
        hs, kn, vn, ps, cs = _layer(hs, past_len, state_pool[l], state_conv[l], k_past, v_past, *w)
        kp_l.append(kp); vp_l.append(vp); pp_l.append(pp); cp_l.append(cp)
        ks_l.append(kn); vs_l.append(vn); ps_l.append(ps); cs_l.append(cs)
    y_prompt = _rmsnorm(hp, norm_final)
    y_sample = _rmsnorm(hs, norm_final)
    return (y_prompt, y_sample,
            jnp.stack(kp_l), jnp.stack(vp_l), jnp.stack(pp_l), jnp.stack(cp_l),
            jnp.stack(ks_l), jnp.stack(vs_l), jnp.stack(ps_l), jnp.stack(cs_l))
```

```python
import functools

import jax
import jax.numpy as jnp
from jax import lax
from jax.experimental import pallas as pl
from jax.experimental.pallas import tpu as pltpu

EPS = 1e-6
HEAD_DIM = 64
LANES = 128
HEADS_PER_GROUP = LANES // HEAD_DIM
POOL_WINDOWS = (2, 4, 8, 16)
POOL_STATE = max(POOL_WINDOWS) - 1
POOL_HALO = 16
CONV_WIDTH = 3
CONV_STATE = CONV_WIDTH - 1
CONV_HALO = 8
VMEM_LIMIT = 56 * 1024 * 1024
BF16 = jnp.bfloat16
F32 = jnp.float32


def _cparams(*sem):
    return pltpu.CompilerParams(dimension_semantics=sem, vmem_limit_bytes=VMEM_LIMIT)


def _const_spec(shape):
    nd = len(shape)
    return pl.BlockSpec(shape, lambda *_: (0,) * nd)


def _rmsnorm(x, g):
    return x * lax.rsqrt(jnp.mean(x * x, axis=-1, keepdims=True) + EPS) * g


def _dot(a, b):
    return jnp.dot(a, b, preferred_element_type=F32)


def _softplus(z):
    return jnp.maximum(z, 0.0) + jnp.log(1.0 + jnp.exp(-jnp.abs(z)))


def _in_proj_kernel(x_ref, g_ref, w_ref, u_ref, q_ref, k_ref, v_ref, ga_ref, gb_ref, *kv_refs,
                    pw, aw, dm):
    h = _rmsnorm(x_ref[...], g_ref[...]).astype(BF16)
    o1, o2, o3, o4, o5 = pw, pw + aw, pw + 2 * aw, pw + 3 * aw, pw + 3 * aw + dm
    u_ref[...] = _dot(h, w_ref[:, :o1])
    q_ref[...] = (_dot(h, w_ref[:, o1:o2]) * (HEAD_DIM ** -0.5)).astype(q_ref.dtype)
    k = _dot(h, w_ref[:, o2:o3])
    v = _dot(h, w_ref[:, o3:o4])
    k_ref[...] = k
    v_ref[...] = v
    ga_ref[...] = jax.nn.sigmoid(_dot(h, w_ref[:, o4:o5]))
    gb_ref[...] = jax.nn.sigmoid(_dot(h, w_ref[:, o5:]))
    if kv_refs:
        kt_ref, vb_ref = kv_refs
        kt_ref[0] = k.T.astype(BF16)
        vb_ref[...] = v.astype(BF16)


def _in_proj(x2d, norm_g, w_in_bf, *, tm, seq_len, pw, aw, dm, q_dtype):
    n, d = x2d.shape
    inw = w_in_bf.shape[1]
    grid = (n // tm,)
    row = lambda w: pl.BlockSpec((tm, w), lambda i: (i, 0))
    out_shape = [jax.ShapeDtypeStruct((n, pw), F32), jax.ShapeDtypeStruct((n, aw), q_dtype),
                 jax.ShapeDtypeStruct((n, aw), F32), jax.ShapeDtypeStruct((n, aw), F32),
                 jax.ShapeDtypeStruct((n, dm), F32), jax.ShapeDtypeStruct((n, dm), F32)]
    out_specs = [row(pw), row(aw), row(aw), row(aw), row(dm), row(dm)]
    if seq_len is not None:
        tps = seq_len // tm
        out_shape += [jax.ShapeDtypeStruct((n // seq_len, aw, seq_len), BF16),
                      jax.ShapeDtypeStruct((n, aw), BF16)]
        out_specs += [pl.BlockSpec((1, aw, tm), lambda i: (i // tps, 0, i % tps)), row(aw)]
    return pl.pallas_call(
        functools.partial(_in_proj_kernel, pw=pw, aw=aw, dm=dm),
        grid=grid,
        in_specs=[row(d), _const_spec((1, d)), _const_spec((d, inw))],
        out_specs=out_specs,
        out_shape=out_shape,
        compiler_params=_cparams("parallel"),
        name="in_proj",
    )(x2d, norm_g.reshape(1, d), w_in_bf)


def _sb_tile(z, tri, carry, mask):
    sp = _softplus(z)
    if mask is not None:
        sp = jnp.where(mask, sp, 0.0)
    right = _dot(sp.astype(BF16), tri)
    w = jnp.exp(z - sp - right - carry)
    if mask is not None:
        w = jnp.where(mask, w, 0.0)
    return w, jnp.sum(sp, axis=1, keepdims=True)


def _prompt_attn_kernel(bias_ref, q_ref, kt_ref, v_ref, tri_ref, o_ref, acc_ref, carry_ref, *, blk):
    g = pl.program_id(1)
    i = pl.program_id(2)
    q2 = q_ref[0]
    lane = lax.broadcasted_iota(jnp.int32, q2.shape, 1)
    zero = jnp.zeros_like(q2)
    qh = [jnp.where(lane < HEAD_DIM, q2, zero), jnp.where(lane >= HEAD_DIM, q2, zero)]
    bias = [bias_ref[g * HEADS_PER_GROUP + h] for h in range(HEADS_PER_GROUP)]
    tri = tri_ref[...]
    acc_ref[...] = jnp.zeros_like(acc_ref)
    carry_ref[...] = jnp.zeros_like(carry_ref)

    def step(j, mask):
        start = pl.multiple_of(j * blk, blk)
        kt = kt_ref[0, :, pl.ds(start, blk)]
        vv = v_ref[0, pl.ds(start, blk), :]
        for h in range(HEADS_PER_GROUP):
            z = _dot(qh[h], kt) + bias[h]
            w, tot = _sb_tile(z, tri, carry_ref[h], mask)
            acc_ref[h] += _dot(w.astype(BF16), vv)
            carry_ref[h] += tot

    row = lax.broadcasted_iota(jnp.int32, (blk, blk), 0)
    col = lax.broadcasted_iota(jnp.int32, (blk, blk), 1)
    step(i, col < row)

    def body(it, c):
        step(i - 1 - it, None)
        return c

    lax.fori_loop(0, i, body, 0)
    lane_o = lax.broadcasted_iota(jnp.int32, (blk, LANES), 1)
    o_ref[0] = jnp.where(lane_o < HEAD_DIM, acc_ref[0], acc_ref[1]).astype(o_ref.dtype)


def _strict_tri(n):
    r = lax.broadcasted_iota(jnp.int32, (n, n), 0)
    c = lax.broadcasted_iota(jnp.int32, (n, n), 1)
    return (r > c).astype(BF16)


def _prompt_attention(q_bf, kt_bf, v_bf, sb_bias, *, blk):
    b, t, aw = q_bf.shape
    groups = aw // LANES
    return pl.pallas_call(
        functools.partial(_prompt_attn_kernel, blk=blk),
        grid=(b, groups, t // blk),
        in_specs=[pl.BlockSpec(memory_space=pltpu.SMEM),
                  pl.BlockSpec((1, blk, LANES), lambda bi, g, i: (bi, i, g)),
                  pl.BlockSpec((1, LANES, t), lambda bi, g, i: (bi, g, 0)),
                  pl.BlockSpec((1, t, LANES), lambda bi, g, i: (bi, 0, g)),
                  _const_spec((blk, blk))],
        out_specs=pl.BlockSpec((1, blk, LANES), lambda bi, g, i: (bi, i, g)),
        out_shape=jax.ShapeDtypeStruct((b, t, aw), BF16),
        scratch_shapes=[pltpu.VMEM((HEADS_PER_GROUP, blk, LANES), F32),
                        pltpu.VMEM((HEADS_PER_GROUP, blk, 1), F32)],
        compiler_params=_cparams("parallel", "parallel", "parallel"),
        name="prompt_attn",
    )(sb_bias, q_bf, kt_bf, v_bf, _strict_tri(blk))


def _decode_attn_kernel(pt_ref, q_ref, kn_ref, vn_ref, bias_ref, tri_ref, *rest,
                        pages_per_step, page, n_heads, dec_seq, chunk):
    k_refs = rest[:pages_per_step]
    v_refs = rest[pages_per_step:2 * pages_per_step]
    o_ref = rest[2 * pages_per_step]
    qbd_ref, kbuf_ref, vbuf_ref, acc_ref, carry_ref = rest[2 * pages_per_step + 1:]
    j = pl.program_id(1)
    rows = n_heads * dec_seq
    aw = n_heads * HEAD_DIM
    bias = bias_ref[...][:, :1]
    tri = tri_ref[...]

    def tile(kk, vv, mask):
        z = lax.dot_general(qbd_ref[...], kk, (((1,), (1,)), ((), ())),
                            preferred_element_type=F32) + bias
        w, tot = _sb_tile(z, tri[:kk.shape[0], :kk.shape[0]], carry_ref[...], mask)
        acc_ref[...] += _dot(w.astype(BF16), vv)
        carry_ref[...] += tot

    @pl.when(j == 0)
    def _():
        q = q_ref[0]
        qt = jnp.concatenate([q] * n_heads, axis=0)
        r = lax.broadcasted_iota(jnp.int32, (rows, aw), 0)
        c = lax.broadcasted_iota(jnp.int32, (rows, aw), 1)
        qbd_ref[...] = jnp.where(c // HEAD_DIM == r // dec_seq, qt, 0.0).astype(BF16)
        acc_ref[...] = jnp.zeros_like(acc_ref)
        carry_ref[...] = jnp.zeros_like(carry_ref)
        pad = jnp.zeros((page - dec_seq, aw), F32)
        kbuf_ref[:page, :] = jnp.concatenate([kn_ref[0], pad], axis=0).astype(BF16)
        vbuf_ref[:page, :] = jnp.concatenate([vn_ref[0], pad], axis=0).astype(BF16)
        rq = lax.broadcasted_iota(jnp.int32, (rows, page), 0) % dec_seq
        ck = lax.broadcasted_iota(jnp.int32, (rows, page), 1)
        tile(kbuf_ref[:page, :], vbuf_ref[:page, :], ck < rq)

    for p in range(pages_per_step):
        dst = (pages_per_step - 1 - p) * page
        kbuf_ref[dst:dst + page, :] = k_refs[p][0].astype(BF16)
        vbuf_ref[dst:dst + page, :] = v_refs[p][0].astype(BF16)
    keys = pages_per_step * page
    for c0 in range(keys - chunk, -1, -chunk):
        tile(kbuf_ref[c0:c0 + chunk, :], vbuf_ref[c0:c0 + chunk, :], None)

    @pl.when(j == pl.num_programs(1) - 1)
    def _():
        acc = acc_ref[...]
        r = lax.broadcasted_iota(jnp.int32, (rows, aw), 0)
        c = lax.broadcasted_iota(jnp.int32, (rows, aw), 1)
        sel = jnp.where(c // HEAD_DIM == r // dec_seq, acc, 0.0)
        out = sel[:dec_seq]
        for h in range(1, n_heads):
            out = out + sel[h * dec_seq:(h + 1) * dec_seq]
        o_ref[0] = out.astype(o_ref.dtype)


def _decode_attention(q_s, k_new, v_new, cache_k, cache_v, page_table, sb_bias, *, pages_per_step):
    bs, ts, aw = q_s.shape
    n_phys, page, _ = cache_k.shape
    n_pages = page_table.shape[1]
    n_heads = aw // HEAD_DIM
    rows = n_heads * ts
    steps = n_pages // pages_per_step
    chunk = min(2 * page, pages_per_step * page)
    bias_rows = jnp.broadcast_to(jnp.repeat(sb_bias, ts)[:, None], (rows, LANES)).astype(F32)

    def page_spec(p):
        def imap(b, j, pt):
            return (pt[b * n_pages + (n_pages - 1 - (j * pages_per_step + p))], 0, 0)
        return pl.BlockSpec((1, page, aw), imap)

    seq_spec = pl.BlockSpec((1, ts, aw), lambda b, j, pt: (b, 0, 0))
    grid_spec = pltpu.PrefetchScalarGridSpec(
        num_scalar_prefetch=1,
        grid=(bs, steps),
        in_specs=[seq_spec, seq_spec, seq_spec,
                  pl.BlockSpec((rows, LANES), lambda b, j, pt: (0, 0)),
                  pl.BlockSpec((chunk, chunk), lambda b, j, pt: (0, 0))]
                 + [page_spec(p) for p in range(pages_per_step)] * 2,
        out_specs=seq_spec,
        scratch_shapes=[pltpu.VMEM((rows, aw), BF16),
                        pltpu.VMEM((pages_per_step * page, aw), BF16),
                        pltpu.VMEM((pages_per_step * page, aw), BF16),
                        pltpu.VMEM((rows, aw), F32),
                        pltpu.VMEM((rows, 1), F32)],
    )
    return pl.pallas_call(
        functools.partial(_decode_attn_kernel, pages_per_step=pages_per_step, page=page,
                          n_heads=n_heads, dec_seq=ts, chunk=chunk),
        grid_spec=grid_spec,
        out_shape=jax.ShapeDtypeStruct((bs, ts, aw), F32),
        compiler_params=_cparams("parallel", "arbitrary"),
        name="decode_attn",
    )(page_table.reshape(-1), q_s, k_new, v_new, bias_rows, _strict_tri(chunk),
      *([cache_k] * pages_per_step), *([cache_v] * pages_per_step))


def _mix_tail(x, y_pool_groups, yb, ga, gb, pw_ref, ps_ref, wpa_ref, wpb_ref, wout_ref):
    mixed = [_dot(p.astype(BF16), pw_ref[g]) for g, p in enumerate(y_pool_groups)]
    y_a = jnp.concatenate(mixed, axis=1) * ps_ref[...]
    m = ga * _dot(y_a.astype(BF16), wpa_ref[...]) + gb * _dot(yb.astype(BF16), wpb_ref[...])
    return x + _dot(m.astype(BF16), wout_ref[...])


def _mix_prompt_kernel(x_ref, u_ref, yb_ref, ga_ref, gb_ref, pw_ref, ps_ref, wpa_ref, wpb_ref,
                       wout_ref, o_ref, st_ref, halo_ref, *, tps, tm):
    i = pl.program_id(0)
    first = (i % tps) == 0

    @pl.when(first)
    def _():
        halo_ref[...] = jnp.zeros_like(halo_ref)

    u = u_ref[...]
    ext = jnp.concatenate([halo_ref[...], u], axis=0)
    pos = (i % tps) * tm + lax.broadcasted_iota(jnp.int32, (tm, LANES), 0)
    groups = []
    for g, w in enumerate(POOL_WINDOWS):
        e = ext[:, g * LANES:(g + 1) * LANES]
        s, span = e, 1
        while span < w:
            s = s + pltpu.roll(s, span, 0)
            span *= 2
        cnt = jnp.minimum(pos + 1, w).astype(F32)
        groups.append(s[POOL_HALO:] / cnt - e[POOL_HALO:])
    o_ref[...] = _mix_tail(x_ref[...], groups, yb_ref[...], ga_ref[...], gb_ref[...],
                           pw_ref, ps_ref, wpa_ref, wpb_ref, wout_ref)
    tail = u[tm - POOL_HALO:]
    halo_ref[...] = tail
    st_ref[0] = tail


def _mix_prompt(x2d, u, yb, ga, gb, pool_w_bf, pool_scale, wpa, wpb, wout, *, tm, seq_len):
    n, d = x2d.shape
    pw = u.shape[1]
    aw = yb.shape[1]
    tps = seq_len // tm
    row = lambda w: pl.BlockSpec((tm, w), lambda i: (i, 0))
    return pl.pallas_call(
        functools.partial(_mix_prompt_kernel, tps=tps, tm=tm),
        grid=(n // tm,),
        in_specs=[row(d), row(pw), row(aw), row(d), row(d),
                  _const_spec(pool_w_bf.shape), _const_spec((1, pw)),
                  _const_spec(wpa.shape), _const_spec(wpb.shape), _const_spec(wout.shape)],
        out_specs=[row(d), pl.BlockSpec((1, POOL_HALO, pw), lambda i: (i // tps, 0, 0))],
        out_shape=[jax.ShapeDtypeStruct((n, d), F32),
                   jax.ShapeDtypeStruct((n // seq_len, POOL_HALO, pw), F32)],
        scratch_shapes=[pltpu.VMEM((POOL_HALO, pw), F32)],
        compiler_params=_cparams("arbitrary"),
        name="mix_prompt",
    )(x2d, u, yb, ga, gb, pool_w_bf, pool_scale.reshape(1, pw), wpa, wpb, wout)


def _mix_sample_kernel(x_ref, st_ref, u_ref, yb_ref, ga_ref, gb_ref, pw_ref, ps_ref, wpa_ref, wpb_ref,
                       wout_ref, o_ref, *, bs, ts, past_len):
    slabs = [st_ref[j * bs:(j + 1) * bs, :] for j in range(POOL_STATE)]
    slabs += [u_ref[t * bs:(t + 1) * bs, :] for t in range(ts)]
    groups = []
    for g, w in enumerate(POOL_WINDOWS):
        e = [s[:, g * LANES:(g + 1) * LANES] for s in slabs]
        s, span = e, 1
        while span < w:
            s = [s[j] + s[j - span] if j >= span else s[j] for j in range(len(s))]
            span *= 2
        rows = []
        for t in range(ts):
            cnt = float(min(past_len + t + 1, w))
            rows.append(s[POOL_STATE + t] / cnt - e[POOL_STATE + t])
        groups.append(jnp.concatenate(rows, axis=0))
    o_ref[...] = _mix_tail(x_ref[...], groups, yb_ref[...], ga_ref[...], gb_ref[...],
                           pw_ref, ps_ref, wpa_ref, wpb_ref, wout_ref)


def _mix_sample(x_tm, state_tm, u, yb, ga, gb, pool_w_bf, pool_scale, wpa, wpb, wout, *, bs, ts, past_len):
    n, d = x_tm.shape
    pw = u.shape[1]
    args = (x_tm, state_tm, u, yb, ga, gb, pool_w_bf, pool_scale.reshape(1, pw), wpa, wpb, wout)
    return pl.pallas_call(
        functools.partial(_mix_sample_kernel, bs=bs, ts=ts, past_len=past_len),
        grid=(1,),
        in_specs=[_const_spec(a.shape) for a in args],
        out_specs=_const_spec((n, d)),
        out_shape=jax.ShapeDtypeStruct((n, d), F32),
        compiler_params=_cparams("arbitrary"),
        name="mix_sample",
    )(*args)


def _gelu_tanh(x):
    return 0.5 * x * (1.0 + jnp.tanh(0.7978845608028654 * (x + 0.044715 * (x * x * x))))


def _ffn_tail(x, a0, a1, a2, gate, cw_ref, cb_ref, wd_ref, nf_ref):
    conv = cb_ref[...] + a0 * cw_ref[0:1, :] + a1 * cw_ref[1:2, :] + a2 * cw_ref[2:3, :]
    hid = (_gelu_tanh(conv) * gate).astype(BF16)
    x2 = x + _dot(hid, wd_ref[...])
    return _rmsnorm(x2, nf_ref[...])


def _ffn_prompt_kernel(x_ref, g_ref, wa_ref, wb_ref, cw_ref, cb_ref, wd_ref, nf_ref, o_ref, st_ref,
                       halo_ref, *, tps, tm):
    i = pl.program_id(0)

    @pl.when((i % tps) == 0)
    def _():
        halo_ref[...] = jnp.zeros_like(halo_ref)

    x = x_ref[...]
    h = _rmsnorm(x, g_ref[...]).astype(BF16)
    a = _dot(h, wa_ref[...])
    gate = _dot(h, wb_ref[...])
    ext = jnp.concatenate([halo_ref[...], a], axis=0)
    a1 = pltpu.roll(ext, 1, 0)[CONV_HALO:]
    a0 = pltpu.roll(ext, 2, 0)[CONV_HALO:]
    o_ref[...] = _ffn_tail(x, a0, a1, a, gate, cw_ref, cb_ref, wd_ref, nf_ref)
    tail = a[tm - CONV_HALO:]
    halo_ref[...] = tail
    st_ref[0] = tail


def _ffn_prompt(x2d, norm_g, wa, wb, conv_w, conv_b, wd, norm_final, *, tm, seq_len):
    n, d = x2d.shape
    f = wa.shape[1]
    tps = seq_len // tm
    row = pl.BlockSpec((tm, d), lambda i: (i, 0))
    return pl.pallas_call(
        functools.partial(_ffn_prompt_kernel, tps=tps, tm=tm),
        grid=(n // tm,),
        in_specs=[row, _const_spec((1, d)), _const_spec(wa.shape), _const_spec(wb.shape),
                  _const_spec(conv_w.shape), _const_spec((1, f)), _const_spec(wd.shape),
                  _const_spec((1, d))],
        out_specs=[row, pl.BlockSpec((1, CONV_HALO, f), lambda i: (i // tps, 0, 0))],
        out_shape=[jax.ShapeDtypeStruct((n, d), F32),
                   jax.ShapeDtypeStruct((n // seq_len, CONV_HALO, f), F32)],
        scratch_shapes=[pltpu.VMEM((CONV_HALO, f), F32)],
        compiler_params=_cparams("arbitrary"),
        name="ffn_prompt",
    )(x2d, norm_g.reshape(1, d), wa, wb, conv_w, conv_b.reshape(1, f), wd, norm_final.reshape(1, d))


def _ffn_sample_kernel(x_ref, st_ref, g_ref, wa_ref, wb_ref, cw_ref, cb_ref, wd_ref, nf_ref,
                       o_ref, a_ref, *, bs):
    x = x_ref[...]
    n = x.shape[0]
    h = _rmsnorm(x, g_ref[...]).astype(BF16)
    a = _dot(h, wa_ref[...])
    gate = _dot(h, wb_ref[...])
    ext = jnp.concatenate([st_ref[...], a], axis=0)
    a0 = ext[:n]
    a1 = ext[bs:bs + n]
    o_ref[...] = _ffn_tail(x, a0, a1, a, gate, cw_ref, cb_ref, wd_ref, nf_ref)
    a_ref[...] = ext[n:]


def _ffn_sample(x_tm, state_tm, norm_g, wa, wb, conv_w, conv_b, wd, norm_final, *, bs):
    n, d = x_tm.shape
    f = wa.shape[1]
    args = (x_tm, state_tm, norm_g.reshape(1, d), wa, wb, conv_w, conv_b.reshape(1, f), wd,
            norm_final.reshape(1, d))
    return pl.pallas_call(
        functools.partial(_ffn_sample_kernel, bs=bs),
        grid=(1,),
        in_specs=[_const_spec(a.shape) for a in args],
        out_specs=[_const_spec((n, d)), _const_spec((CONV_STATE * bs, f))],
        out_shape=[jax.ShapeDtypeStruct((n, d), F32), jax.ShapeDtypeStruct((CONV_STATE * bs, f), F32)],
        compiler_params=_cparams("arbitrary"),
        name="ffn_sample",
    )(*args)


def _tile(n, pref):
    t = min(pref, n)
    assert n % t == 0, (n, t)
    return t


def kernel(x_prompt, x_sample, cache_k, cache_v, page_table, state_pool, state_conv, norm_mix, w_in, sb_bias, pool_w, pool_scale, w_proj_a, w_proj_b, w_out, norm_ffn, w_up_a, w_up_b, conv_w, conv_b, w_down, norm_final):
    depth = norm_mix.shape[0]
    bp, t, d = x_prompt.shape
    bs, ts, _ = x_sample.shape
    n_heads, hd = cache_k.shape[-2:]
    assert hd == HEAD_DIM
    aw = n_heads * hd
    pw = pool_scale.shape[-1]
    n_phys, page = cache_k.shape[1:3]
    past_len = page_table.shape[1] * page
    assert pw == LANES * len(POOL_WINDOWS) and state_pool.shape[2] == POOL_STATE
    assert state_conv.shape[2] == CONV_STATE and conv_w.shape[1] == CONV_WIDTH

    tm = _tile(t, 256)
    blk = _tile(t, 256)
    pps = _tile(page_table.shape[1], 8)

    hp = x_prompt.reshape(bp * t, d)
    hs = jnp.transpose(x_sample, (1, 0, 2)).reshape(ts * bs, d)
    outs = {k: [] for k in ("kp", "vp", "pp", "cp", "ks", "vs", "ps", "cs")}
    for l in range(depth):
        w_in_bf = w_in[l].astype(BF16)
        pool_w_bf = pool_w[l].astype(BF16)
        wpa, wpb, wout = w_proj_a[l].astype(BF16), w_proj_b[l].astype(BF16), w_out[l].astype(BF16)
        wa, wb, wd = w_up_a[l].astype(BF16), w_up_b[l].astype(BF16), w_down[l].astype(BF16)

        u, q, k, v, ga, gb, kt, vb = _in_proj(hp, norm_mix[l], w_in_bf, tm=tm, seq_len=t,
                                              pw=pw, aw=aw, dm=d, q_dtype=BF16)
        yb = _prompt_attention(q.reshape(bp, t, aw), kt, vb.reshape(bp, t, aw), sb_bias[l], blk=blk)
        hp, pool_tail = _mix_prompt(hp, u, yb.reshape(bp * t, aw), ga, gb, pool_w_bf, pool_scale[l],
                                    wpa, wpb, wout, tm=tm, seq_len=t)
        hp_out, conv_tail = _ffn_prompt(hp, norm_ffn[l], wa, wb, conv_w[l], conv_b[l], wd,
                                        norm_final if l == depth - 1 else jnp.ones_like(norm_final),
                                        tm=tm, seq_len=t)
        outs["kp"].append(k.reshape(bp, t, n_heads, hd))
        outs["vp"].append(v.reshape(bp, t, n_heads, hd))
        outs["pp"].append(pool_tail[:, POOL_HALO - POOL_STATE:])
        outs["cp"].append(conv_tail[:, CONV_HALO - CONV_STATE:])

        us, qs, ksn, vsn, gas, gbs = _in_proj(hs, norm_mix[l], w_in_bf, tm=_tile(ts * bs, 256), seq_len=None,
                                              pw=pw, aw=aw, dm=d, q_dtype=F32)
        to_bm = lambda a: jnp.transpose(a.reshape(ts, bs, -1), (1, 0, 2))
        ks_bm, vs_bm = to_bm(ksn), to_bm(vsn)
        ybs = _decode_attention(to_bm(qs), ks_bm, vs_bm, cache_k[l].reshape(n_phys, page, aw),
                                cache_v[l].reshape(n_phys, page, aw), page_table, sb_bias[l],
                                pages_per_step=pps)
        ybs_tm = jnp.transpose(ybs, (1, 0, 2)).reshape(ts * bs, aw)
        pool_tm = jnp.transpose(state_pool[l], (1, 0, 2)).reshape(POOL_STATE * bs, pw)
        hs = _mix_sample(hs, pool_tm, us, ybs_tm, gas, gbs, pool_w_bf, pool_scale[l], wpa, wpb, wout,
                         bs=bs, ts=ts, past_len=past_len)
        conv_tm = jnp.transpose(state_conv[l], (1, 0, 2)).reshape(CONV_STATE * bs, -1)
        hs_out, a_tail = _ffn_sample(hs, conv_tm, norm_ffn[l], wa, wb, conv_w[l], conv_b[l], wd,
                                     norm_final if l == depth - 1 else jnp.ones_like(norm_final), bs=bs)
        outs["ks"].append(ks_bm.reshape(bs, ts, n_heads, hd))
        outs["vs"].append(vs_bm.reshape(bs, ts, n_heads, hd))
        pool_ext = jnp.concatenate([state_pool[l], to_bm(us)], axis=1)
        outs["ps"].append(pool_ext[:, -POOL_STATE:])
        outs["cs"].append(jnp.transpose(a_tail.reshape(CONV_STATE, bs, -1), (1, 0, 2)))
        assert depth == 1, "deeper stacks need the un-normalised FFN output carried between layers"

    y_prompt = hp_out.reshape(bp, t, d)
    y_sample = jnp.transpose(hs_out.reshape(ts, bs, d), (1, 0, 2))
    st = lambda key: jnp.stack(outs[key])
    return (y_prompt, y_sample, st("kp"), st("vp"), st("pp"), st("cp"),
            st("ks"), st("vs"), st("ps"), st("cs"))
```

```python
import functools
import math

import jax
import jax.numpy as jnp
from jax import lax
from jax.experimental import pallas as pl
from jax.experimental.pallas import tpu as pltpu

EPS = 1e-6
HEAD_DIM = 64
LANES = 128
HEADS_PER_GROUP = LANES // HEAD_DIM
POOL_WINDOWS = (2, 4, 8, 16)
POOL_STATE = max(POOL_WINDOWS) - 1
POOL_HALO = 16
CONV_WIDTH = 3
CONV_STATE = CONV_WIDTH - 1
CONV_HALO = 8
VMEM_LIMIT = 56 * 1024 * 1024
BF16 = jnp.bfloat16
F32 = jnp.float32
LOG2E = math.log2(math.e)
Q_SCALE = HEAD_DIM ** -0.5 * LOG2E


def _cparams(*sem):
    return pltpu.CompilerParams(dimension_semantics=sem, vmem_limit_bytes=VMEM_LIMIT)


def _const_spec(shape):
    nd = len(shape)
    return pl.BlockSpec(shape, lambda *_: (0,) * nd)


def _rmsnorm(x, g):
    return x * lax.rsqrt(jnp.mean(x * x, axis=-1, keepdims=True) + EPS) * g


def _dot(a, b):
    return jnp.dot(a, b, preferred_element_type=F32)


def _softplus2(z):
    neg_abs = pltpu.bitcast(pltpu.bitcast(z, jnp.int32) | jnp.int32(-2 ** 31), F32)
    return jnp.maximum(z, 0.0) + jnp.log2(1.0 + jnp.exp2(neg_abs))


def _in_proj_kernel(x_ref, g_ref, w_ref, u_ref, q_ref, k_ref, v_ref, ga_ref, gb_ref, *kv_refs,
                    pw, aw, dm):
    h = _rmsnorm(x_ref[...], g_ref[...]).astype(BF16)
    o1, o2, o3, o4, o5 = pw, pw + aw, pw + 2 * aw, pw + 3 * aw, pw + 3 * aw + dm
    u_ref[...] = _dot(h, w_ref[:, :o1])
    q_ref[...] = (_dot(h, w_ref[:, o1:o2]) * Q_SCALE).astype(q_ref.dtype)
    k = _dot(h, w_ref[:, o2:o3])
    v = _dot(h, w_ref[:, o3:o4])
    ga_ref[...] = jax.nn.sigmoid(_dot(h, w_ref[:, o4:o5]))
    gb_ref[...] = jax.nn.sigmoid(_dot(h, w_ref[:, o5:]))
    if kv_refs:
        ktb_ref, vb_ref = kv_refs
        kt = k.T
        k_ref[0] = kt
        v_ref[0] = v.T
        ktb_ref[0] = kt.astype(BF16)
        vb_ref[...] = v.astype(BF16)
    else:
        k_ref[...] = k
        v_ref[...] = v


def _in_proj(x2d, norm_g, w_in_bf, *, tm, seq_len, pw, aw, dm, q_dtype):
    n, d = x2d.shape
    inw = w_in_bf.shape[1]
    grid = (n // tm,)
    row = lambda w: pl.BlockSpec((tm, w), lambda i: (i, 0))
    kv_shape, kv_spec = jax.ShapeDtypeStruct((n, aw), F32), row(aw)
    extra_shape, extra_specs = [], []
    if seq_len is not None:
        tps = seq_len // tm
        kv_shape = jax.ShapeDtypeStruct((n // seq_len, aw, seq_len), F32)
        kv_spec = pl.BlockSpec((1, aw, tm), lambda i: (i // tps, 0, i % tps))
        extra_shape = [jax.ShapeDtypeStruct((n // seq_len, aw, seq_len), BF16),
                       jax.ShapeDtypeStruct((n, aw), BF16)]
        extra_specs = [kv_spec, row(aw)]
    out_shape = [jax.ShapeDtypeStruct((n, pw), F32), jax.ShapeDtypeStruct((n, aw), q_dtype),
                 kv_shape, kv_shape,
                 jax.ShapeDtypeStruct((n, dm), F32), jax.ShapeDtypeStruct((n, dm), F32)] + extra_shape
    out_specs = [row(pw), row(aw), kv_spec, kv_spec, row(dm), row(dm)] + extra_specs
    return pl.pallas_call(
        functools.partial(_in_proj_kernel, pw=pw, aw=aw, dm=dm),
        grid=grid,
        in_specs=[row(d), _const_spec((1, d)), _const_spec((d, inw))],
        out_specs=out_specs,
        out_shape=out_shape,
        compiler_params=_cparams("parallel"),
        name="in_proj",
    )(x2d, norm_g.reshape(1, d), w_in_bf)


def _sb_tile(z, tri, carry, mask):
    sp = _softplus2(z)
    if mask is not None:
        sp = jnp.where(mask, sp, 0.0)
    right = _dot(sp.astype(BF16), tri)
    w = jnp.exp2(z - sp - right - carry)
    if mask is not None:
        w = jnp.where(mask, w, 0.0)
    return w, jnp.sum(sp, axis=1, keepdims=True)


def _prompt_attn_kernel(bias_ref, q_ref, kt_ref, v_ref, tri_ref, o_ref, acc_ref, carry_ref, *, tq, tk):
    g = pl.program_id(1)
    i = pl.program_id(2)
    q2 = q_ref[0]
    lane = lax.broadcasted_iota(jnp.int32, q2.shape, 1)
    zero = jnp.zeros_like(q2)
    qh = [jnp.where(lane < HEAD_DIM, q2, zero), jnp.where(lane >= HEAD_DIM, q2, zero)]
    bias = [bias_ref[g * HEADS_PER_GROUP + h] * LOG2E for h in range(HEADS_PER_GROUP)]
    tri = tri_ref[...]
    acc_ref[...] = jnp.zeros_like(acc_ref)
    carry_ref[...] = jnp.zeros_like(carry_ref)
    ratio = tq // tk

    def step(j, r0, masked):
        start = pl.multiple_of(j * tk, tk)
        kt = kt_ref[0, :, pl.ds(start, tk)]
        vv = v_ref[0, pl.ds(start, tk), :]
        mask = None
        if masked:
            row = lax.broadcasted_iota(jnp.int32, (tq - r0, tk), 0)
            col = lax.broadcasted_iota(jnp.int32, (tq - r0, tk), 1)
            mask = col < row
        for h in range(HEADS_PER_GROUP):
            z = _dot(qh[h][r0:], kt) + bias[h]
            w, tot = _sb_tile(z, tri, carry_ref[h, r0:], mask)
            acc_ref[h, r0:] += _dot(w.astype(BF16), vv)
            carry_ref[h, r0:] += tot

    for c in range(ratio - 1, -1, -1):
        step(i * ratio + c, c * tk, True)

    def body(it, carry):
        step(i * ratio - 1 - it, 0, False)
        return carry

    lax.fori_loop(0, i * ratio, body, 0)
    lane_o = lax.broadcasted_iota(jnp.int32, (tq, LANES), 1)
    o_ref[0] = jnp.where(lane_o < HEAD_DIM, acc_ref[0], acc_ref[1]).astype(o_ref.dtype)


def _strict_tri(n):
    r = lax.broadcasted_iota(jnp.int32, (n, n), 0)
    c = lax.broadcasted_iota(jnp.int32, (n, n), 1)
    return (r > c).astype(BF16)


def _prompt_attention(q_bf, kt_bf, v_bf, sb_bias, *, tq, tk):
    b, t, aw = q_bf.shape
    groups = aw // LANES
    return pl.pallas_call(
        functools.partial(_prompt_attn_kernel, tq=tq, tk=tk),
        grid=(b, groups, t // tq),
        in_specs=[pl.BlockSpec(memory_space=pltpu.SMEM),
                  pl.BlockSpec((1, tq, LANES), lambda bi, g, i: (bi, i, g)),
                  pl.BlockSpec((1, LANES, t), lambda bi, g, i: (bi, g, 0)),
                  pl.BlockSpec((1, t, LANES), lambda bi, g, i: (bi, 0, g)),
                  _const_spec((tk, tk))],
        out_specs=pl.BlockSpec((1, tq, LANES), lambda bi, g, i: (bi, i, g)),
        out_shape=jax.ShapeDtypeStruct((b, t, aw), BF16),
        scratch_shapes=[pltpu.VMEM((HEADS_PER_GROUP, tq, LANES), F32),
                        pltpu.VMEM((HEADS_PER_GROUP, tq, 1), F32)],
        compiler_params=_cparams("parallel", "parallel", "parallel"),
        name="prompt_attn",
    )(sb_bias, q_bf, kt_bf, v_bf, _strict_tri(tk))


def _decode_attn_kernel(pt_ref, q_ref, kn_ref, vn_ref, bias_ref, tri_ref, *rest,
                        pages_per_step, page, n_heads, dec_seq, chunk):
    k_refs = rest[:pages_per_step]
    v_refs = rest[pages_per_step:2 * pages_per_step]
    o_ref = rest[2 * pages_per_step]
    qbd_ref, kbuf_ref, vbuf_ref, acc_ref, carry_ref = rest[2 * pages_per_step + 1:]
    j = pl.program_id(1)
    rows = n_heads * dec_seq
    aw = n_heads * HEAD_DIM
    bias = bias_ref[...][:, :1]
    tri = tri_ref[...]

    def tile(kk_t, vv_t, mask):
        n = kk_t.shape[1]
        z = _dot(qbd_ref[...], kk_t) + bias
        w, tot = _sb_tile(z, tri[:n, :n], carry_ref[...], mask)
        acc_ref[...] += lax.dot_general(w.astype(BF16), vv_t, (((1,), (1,)), ((), ())),
                                        preferred_element_type=F32)
        carry_ref[...] += tot

    @pl.when(j == 0)
    def _():
        q = q_ref[0]
        qt = jnp.concatenate([q] * n_heads, axis=0)
        r = lax.broadcasted_iota(jnp.int32, (rows, aw), 0)
        c = lax.broadcasted_iota(jnp.int32, (rows, aw), 1)
        qbd_ref[...] = jnp.where(c // HEAD_DIM == r // dec_seq, qt, 0.0).astype(BF16)
        acc_ref[...] = jnp.zeros_like(acc_ref)
        carry_ref[...] = jnp.zeros_like(carry_ref)
        pad = jnp.zeros((page - dec_seq, aw), F32)
        kn_t = jnp.concatenate([kn_ref[0], pad], axis=0).T.astype(BF16)
        vn_t = jnp.concatenate([vn_ref[0], pad], axis=0).T.astype(BF16)
        rq = lax.broadcasted_iota(jnp.int32, (rows, page), 0) % dec_seq
        ck = lax.broadcasted_iota(jnp.int32, (rows, page), 1)
        tile(kn_t, vn_t, ck < rq)

    for p in range(pages_per_step):
        dst = (pages_per_step - 1 - p) * page
        kbuf_ref[:, dst:dst + page] = k_refs[p][0].astype(BF16)
        vbuf_ref[:, dst:dst + page] = v_refs[p][0].astype(BF16)
    keys = pages_per_step * page
    z = _dot(qbd_ref[...], kbuf_ref[...]) + bias
    sp = _softplus2(z)
    starts = list(range(0, keys, chunk))
    sp_c = [sp[:, c0:c0 + chunk] for c0 in starts]
    right = [_dot(s.astype(BF16), tri) for s in sp_c]
    totals = [jnp.sum(s, axis=1, keepdims=True) for s in sp_c]
    carry = carry_ref[...]
    ws = [None] * len(starts)
    for ci in range(len(starts) - 1, -1, -1):
        c0 = starts[ci]
        ws[ci] = jnp.exp2(z[:, c0:c0 + chunk] - sp_c[ci] - right[ci] - carry).astype(BF16)
        carry = carry + totals[ci]
    acc_ref[...] += lax.dot_general(jnp.concatenate(ws, axis=1), vbuf_ref[...],
                                    (((1,), (1,)), ((), ())), preferred_element_type=F32)
    carry_ref[...] = carry

    @pl.when(j == pl.num_programs(1) - 1)
    def _():
        acc = acc_ref[...]
        r = lax.broadcasted_iota(jnp.int32, (rows, aw), 0)
        c = lax.broadcasted_iota(jnp.int32, (rows, aw), 1)
        sel = jnp.where(c // HEAD_DIM == r // dec_seq, acc, 0.0)
        out = sel[:dec_seq]
        for h in range(1, n_heads):
            out = out + sel[h * dec_seq:(h + 1) * dec_seq]
        o_ref[0] = out.astype(o_ref.dtype)


def _decode_attention(q_s, k_new, v_new, cache_k, cache_v, page_table, sb_bias, *, pages_per_step):
    bs, ts, aw = q_s.shape
    n_phys, _, page = cache_k.shape
    n_pages = page_table.shape[1]
    n_heads = aw // HEAD_DIM
    rows = n_heads * ts
    steps = n_pages // pages_per_step
    chunk = min(2 * page, pages_per_step * page)
    bias_rows = jnp.broadcast_to(jnp.repeat(sb_bias * LOG2E, ts)[:, None], (rows, LANES)).astype(F32)

    def page_spec(p):
        def imap(b, j, pt):
            return (pt[b * n_pages + (n_pages - 1 - (j * pages_per_step + p))], 0, 0)
        return pl.BlockSpec((1, aw, page), imap)

    seq_spec = pl.BlockSpec((1, ts, aw), lambda b, j, pt: (b, 0, 0))
    grid_spec = pltpu.PrefetchScalarGridSpec(
        num_scalar_prefetch=1,
        grid=(bs, steps),
        in_specs=[seq_spec, seq_spec, seq_spec,
                  pl.BlockSpec((rows, LANES), lambda b, j, pt: (0, 0)),
                  pl.BlockSpec((chunk, chunk), lambda b, j, pt: (0, 0))]
                 + [page_spec(p) for p in range(pages_per_step)] * 2,
        out_specs=seq_spec,
        scratch_shapes=[pltpu.VMEM((rows, aw), BF16),
                        pltpu.VMEM((aw, pages_per_step * page), BF16),
                        pltpu.VMEM((aw, pages_per_step * page), BF16),
                        pltpu.VMEM((rows, aw), F32),
                        pltpu.VMEM((rows, 1), F32)],
    )
    return pl.pallas_call(
        functools.partial(_decode_attn_kernel, pages_per_step=pages_per_step, page=page,
                          n_heads=n_heads, dec_seq=ts, chunk=chunk),
        grid_spec=grid_spec,
        out_shape=jax.ShapeDtypeStruct((bs, ts, aw), F32),
        compiler_params=_cparams("parallel", "arbitrary"),
        name="decode_attn",
    )(page_table.reshape(-1), q_s, k_new, v_new, bias_rows, _strict_tri(chunk),
      *([cache_k] * pages_per_step), *([cache_v] * pages_per_step))


def _mix_tail(x, y_pool_groups, yb, ga, gb, pw_ref, ps_ref, wpa_ref, wpb_ref, wout_ref):
    mixed = [_dot(p.astype(BF16), pw_ref[g]) for g, p in enumerate(y_pool_groups)]
    y_a = jnp.concatenate(mixed, axis=1) * ps_ref[...]
    m = ga * _dot(y_a.astype(BF16), wpa_ref[...]) + gb * _dot(yb.astype(BF16), wpb_ref[...])
    return x + _dot(m.astype(BF16), wout_ref[...])


def _mix_prompt_kernel(x_ref, u_ref, yb_ref, ga_ref, gb_ref, pw_ref, ps_ref, wpa_ref, wpb_ref,
                       wout_ref, o_ref, st_ref, halo_ref, *, tps, tm):
    i = pl.program_id(0)
    first = (i % tps) == 0

    @pl.when(first)
    def _():
        halo_ref[...] = jnp.zeros_like(halo_ref)

    u = u_ref[...]
    ext = jnp.concatenate([halo_ref[...], u], axis=0)
    pos = (i % tps) * tm + lax.broadcasted_iota(jnp.int32, (tm, LANES), 0)
    groups = []
    for g, w in enumerate(POOL_WINDOWS):
        e = ext[:, g * LANES:(g + 1) * LANES]
        s, span = e, 1
        while span < w:
            s = s + pltpu.roll(s, span, 0)
            span *= 2
        cnt = jnp.minimum(pos + 1, w).astype(F32)
        groups.append(s[POOL_HALO:] / cnt - e[POOL_HALO:])
    o_ref[...] = _mix_tail(x_ref[...], groups, yb_ref[...], ga_ref[...], gb_ref[...],
                           pw_ref, ps_ref, wpa_ref, wpb_ref, wout_ref)
    tail = u[tm - POOL_HALO:]
    halo_ref[...] = tail
    st_ref[0] = tail


def _mix_prompt(x2d, u, yb, ga, gb, pool_w_bf, pool_scale, wpa, wpb, wout, *, tm, seq_len):
    n, d = x2d.shape
    pw = u.shape[1]
    aw = yb.shape[1]
    tps = seq_len // tm
    row = lambda w: pl.BlockSpec((tm, w), lambda i: (i, 0))
    return pl.pallas_call(
        functools.partial(_mix_prompt_kernel, tps=tps, tm=tm),
        grid=(n // tm,),
        in_specs=[row(d), row(pw), row(aw), row(d), row(d),
                  _const_spec(pool_w_bf.shape), _const_spec((1, pw)),
                  _const_spec(wpa.shape), _const_spec(wpb.shape), _const_spec(wout.shape)],
        out_specs=[row(d), pl.BlockSpec((1, POOL_HALO, pw), lambda i: (i // tps, 0, 0))],
        out_shape=[jax.ShapeDtypeStruct((n, d), F32),
                   jax.ShapeDtypeStruct((n // seq_len, POOL_HALO, pw), F32)],
        scratch_shapes=[pltpu.VMEM((POOL_HALO, pw), F32)],
        compiler_params=_cparams("arbitrary"),
        name="mix_prompt",
    )(x2d, u, yb, ga, gb, pool_w_bf, pool_scale.reshape(1, pw), wpa, wpb, wout)


def _mix_sample_kernel(x_ref, st_ref, u_ref, yb_ref, ga_ref, gb_ref, pw_ref, ps_ref, wpa_ref, wpb_ref,
                       wout_ref, o_ref, *, bs, ts, past_len):
    slabs = [st_ref[j * bs:(j + 1) * bs, :] for j in range(POOL_STATE)]
    slabs += [u_ref[t * bs:(t + 1) * bs, :] for t in range(ts)]
    groups = []
    for g, w in enumerate(POOL_WINDOWS):
        e = [s[:, g * LANES:(g + 1) * LANES] for s in slabs]
        s, span = e, 1
        while span < w:
            s = [s[j] + s[j - span] if j >= span else s[j] for j in range(len(s))]
            span *= 2
        rows = []
        for t in range(ts):
            cnt = float(min(past_len + t + 1, w))
            rows.append(s[POOL_STATE + t] / cnt - e[POOL_STATE + t])
        groups.append(jnp.concatenate(rows, axis=0))
    o_ref[...] = _mix_tail(x_ref[...], groups, yb_ref[...], ga_ref[...], gb_ref[...],
                           pw_ref, ps_ref, wpa_ref, wpb_ref, wout_ref)


def _mix_sample(x_tm, state_tm, u, yb, ga, gb, pool_w_bf, pool_scale, wpa, wpb, wout, *, bs, ts, past_len):
    n, d = x_tm.shape
    pw = u.shape[1]
    args = (x_tm, state_tm, u, yb, ga, gb, pool_w_bf, pool_scale.reshape(1, pw), wpa, wpb, wout)
    return pl.pallas_call(
        functools.partial(_mix_sample_kernel, bs=bs, ts=ts, past_len=past_len),
        grid=(1,),
        in_specs=[_const_spec(a.shape) for a in args],
        out_specs=_const_spec((n, d)),
        out_shape=jax.ShapeDtypeStruct((n, d), F32),
        compiler_params=_cparams("arbitrary"),
        name="mix_sample",
    )(*args)


def _gelu_tanh(x):
    return 0.5 * x * (1.0 + jnp.tanh(0.7978845608028654 * (x + 0.044715 * (x * x * x))))


def _ffn_tail(x, a0, a1, a2, gate, cw_ref, cb_ref, wd_ref, nf_ref):
    conv = cb_ref[...] + a0 * cw_ref[0:1, :] + a1 * cw_ref[1:2, :] + a2 * cw_ref[2:3, :]
    hid = (_gelu_tanh(conv) * gate).astype(BF16)
    x2 = x + _dot(hid, wd_ref[...])
    return _rmsnorm(x2, nf_ref[...])


def _ffn_prompt_kernel(x_ref, g_ref, wa_ref, wb_ref, cw_ref, cb_ref, wd_ref, nf_ref, o_ref, st_ref,
                       halo_ref, *, tps, tm):
    i = pl.program_id(0)

    @pl.when((i % tps) == 0)
    def _():
        halo_ref[...] = jnp.zeros_like(halo_ref)

    x = x_ref[...]
    h = _rmsnorm(x, g_ref[...]).astype(BF16)
    a = _dot(h, wa_ref[...])
    gate = _dot(h, wb_ref[...])
    ext = jnp.concatenate([halo_ref[...], a], axis=0)
    a1 = pltpu.roll(ext, 1, 0)[CONV_HALO:]
    a0 = pltpu.roll(ext, 2, 0)[CONV_HALO:]
    o_ref[...] = _ffn_tail(x, a0, a1, a, gate, cw_ref, cb_ref, wd_ref, nf_ref)
    tail = a[tm - CONV_HALO:]
    halo_ref[...] = tail
    st_ref[0] = tail


def _ffn_prompt(x2d, norm_g, wa, wb, conv_w, conv_b, wd, norm_final, *, tm, seq_len):
    n, d = x2d.shape
    f = wa.shape[1]
    tps = seq_len // tm
    row = pl.BlockSpec((tm, d), lambda i: (i, 0))
    return pl.pallas_call(
        functools.partial(_ffn_prompt_kernel, tps=tps, tm=tm),
        grid=(n // tm,),
        in_specs=[row, _const_spec((1, d)), _const_spec(wa.shape), _const_spec(wb.shape),
                  _const_spec(conv_w.shape), _const_spec((1, f)), _const_spec(wd.shape),
                  _const_spec((1, d))],
        out_specs=[row, pl.BlockSpec((1, CONV_HALO, f), lambda i: (i // tps, 0, 0))],
        out_shape=[jax.ShapeDtypeStruct((n, d), F32),
                   jax.ShapeDtypeStruct((n // seq_len, CONV_HALO, f), F32)],
        scratch_shapes=[pltpu.VMEM((CONV_HALO, f), F32)],
        compiler_params=_cparams("arbitrary"),
        name="ffn_prompt",
    )(x2d, norm_g.reshape(1, d), wa, wb, conv_w, conv_b.reshape(1, f), wd, norm_final.reshape(1, d))


def _ffn_sample_kernel(x_ref, st_ref, g_ref, wa_ref, wb_ref, cw_ref, cb_ref, wd_ref, nf_ref,
                       o_ref, a_ref, *, bs):
    x = x_ref[...]
    n = x.shape[0]
    h = _rmsnorm(x, g_ref[...]).astype(BF16)
    a = _dot(h, wa_ref[...])
    gate = _dot(h, wb_ref[...])
    ext = jnp.concatenate([st_ref[...], a], axis=0)
    a0 = ext[:n]
    a1 = ext[bs:bs + n]
    o_ref[...] = _ffn_tail(x, a0, a1, a, gate, cw_ref, cb_ref, wd_ref, nf_ref)
    a_ref[...] = ext[n:]


def _ffn_sample(x_tm, state_tm, norm_g, wa, wb, conv_w, conv_b, wd, norm_final, *, bs):
    n, d = x_tm.shape
    f = wa.shape[1]
    args = (x_tm, state_tm, norm_g.reshape(1, d), wa, wb, conv_w, conv_b.reshape(1, f), wd,
            norm_final.reshape(1, d))
    return pl.pallas_call(
        functools.partial(_ffn_sample_kernel, bs=bs),
        grid=(1,),
        in_specs=[_const_spec(a.shape) for a in args],
        out_specs=[_const_spec((n, d)), _const_spec((CONV_STATE * bs, f))],
        out_shape=[jax.ShapeDtypeStruct((n, d), F32), jax.ShapeDtypeStruct((CONV_STATE * bs, f), F32)],
        compiler_params=_cparams("arbitrary"),
        name="ffn_sample",
    )(*args)


def _tile(n, pref):
    t = min(pref, n)
    assert n % t == 0, (n, t)
    return t


def kernel(x_prompt, x_sample, cache_k, cache_v, page_table, state_pool, state_conv, norm_mix, w_in, sb_bias, pool_w, pool_scale, w_proj_a, w_proj_b, w_out, norm_ffn, w_up_a, w_up_b, conv_w, conv_b, w_down, norm_final):
    depth = norm_mix.shape[0]
    bp, t, d = x_prompt.shape
    bs, ts, _ = x_sample.shape
    n_heads, hd = cache_k.shape[-2:]
    assert hd == HEAD_DIM
    aw = n_heads * hd
    pw = pool_scale.shape[-1]
    n_phys, page = cache_k.shape[1:3]
    past_len = page_table.shape[1] * page
    assert pw == LANES * len(POOL_WINDOWS) and state_pool.shape[2] == POOL_STATE
    assert state_conv.shape[2] == CONV_STATE and conv_w.shape[1] == CONV_WIDTH

    tm = _tile(t, 256)
    tk = _tile(t, 256)
    tq = _tile(t, 512)
    pps = _tile(page_table.shape[1], 16)

    def pages_keys_last(cache):
        return jnp.transpose(cache, (0, 2, 3, 1)).reshape(n_phys, aw, page)

    hp = x_prompt.reshape(bp * t, d)
    hs = jnp.transpose(x_sample, (1, 0, 2)).reshape(ts * bs, d)
    outs = {k: [] for k in ("kp", "vp", "pp", "cp", "ks", "vs", "ps", "cs")}
    for l in range(depth):
        w_in_bf = w_in[l].astype(BF16)
        pool_w_bf = pool_w[l].astype(BF16)
        wpa, wpb, wout = w_proj_a[l].astype(BF16), w_proj_b[l].astype(BF16), w_out[l].astype(BF16)
        wa, wb, wd = w_up_a[l].astype(BF16), w_up_b[l].astype(BF16), w_down[l].astype(BF16)

        u, q, k, v, ga, gb, kt, vb = _in_proj(hp, norm_mix[l], w_in_bf, tm=tm, seq_len=t,
                                              pw=pw, aw=aw, dm=d, q_dtype=BF16)
        yb = _prompt_attention(q.reshape(bp, t, aw), kt, vb.reshape(bp, t, aw), sb_bias[l], tq=tq, tk=tk)
        hp, pool_tail = _mix_prompt(hp, u, yb.reshape(bp * t, aw), ga, gb, pool_w_bf, pool_scale[l],
                                    wpa, wpb, wout, tm=tm, seq_len=t)
        hp_out, conv_tail = _ffn_prompt(hp, norm_ffn[l], wa, wb, conv_w[l], conv_b[l], wd,
                                        norm_final if l == depth - 1 else jnp.ones_like(norm_final),
                                        tm=tm, seq_len=t)
        rows_major = lambda a: jnp.transpose(a.reshape(bp, n_heads, hd, t), (0, 3, 1, 2))
        outs["kp"].append(rows_major(k))
        outs["vp"].append(rows_major(v))
        outs["pp"].append(pool_tail[:, POOL_HALO - POOL_STATE:])
        outs["cp"].append(conv_tail[:, CONV_HALO - CONV_STATE:])

        us, qs, ksn, vsn, gas, gbs = _in_proj(hs, norm_mix[l], w_in_bf, tm=_tile(ts * bs, 256), seq_len=None,
                                              pw=pw, aw=aw, dm=d, q_dtype=F32)
        to_bm = lambda a: jnp.transpose(a.reshape(ts, bs, -1), (1, 0, 2))
        ks_bm, vs_bm = to_bm(ksn), to_bm(vsn)
        ybs = _decode_attention(to_bm(qs), ks_bm, vs_bm, pages_keys_last(cache_k[l]),
                                pages_keys_last(cache_v[l]), page_table, sb_bias[l],
                                pages_per_step=pps)
        ybs_tm = jnp.transpose(ybs, (1, 0, 2)).reshape(ts * bs, aw)
        pool_tm = jnp.transpose(state_pool[l], (1, 0, 2)).reshape(POOL_STATE * bs, pw)
        hs = _mix_sample(hs, pool_tm, us, ybs_tm, gas, gbs, pool_w_bf, pool_scale[l], wpa, wpb, wout,
                         bs=bs, ts=ts, past_len=past_len)
        conv_tm = jnp.transpose(state_conv[l], (1, 0, 2)).reshape(CONV_STATE * bs, -1)
        hs_out, a_tail = _ffn_sample(hs, conv_tm, norm_ffn[l], wa, wb, conv_w[l], conv_b[l], wd,
                                     norm_final if l == depth - 1 else jnp.ones_like(norm_final), bs=bs)
        outs["ks"].append(ks_bm.reshape(bs, ts, n_heads, hd))
        outs["vs"].append(vs_bm.reshape(bs, ts, n_heads, hd))
        pool_ext = jnp.concatenate([state_pool[l], to_bm(us)], axis=1)
        outs["ps"].append(pool_ext[:, -POOL_STATE:])
        outs["cs"].append(jnp.transpose(a_tail.reshape(CONV_STATE, bs, -1), (1, 0, 2)))
        assert depth == 1, "deeper stacks need the un-normalised FFN output carried between layers"

    y_prompt = hp_out.reshape(bp, t, d)
    y_sample = jnp.transpose(hs_out.reshape(ts, bs, d), (1, 0, 2))
    st = lambda key: jnp.stack(outs[key])
    return (y_prompt, y_sample, st("kp"), st("vp"), st("pp"), st("cp"),
            st("ks"), st("vs"), st("ps"), st("cs"))
```

```python
import functools
import math

import jax
import jax.numpy as jnp
from jax import lax
from jax.experimental import pallas as pl
from jax.experimental.pallas import tpu as pltpu

EPS = 1e-6
HEAD_DIM = 64
LANES = 128
HEADS_PER_GROUP = LANES // HEAD_DIM
POOL_WINDOWS = (2, 4, 8, 16)
POOL_STATE = max(POOL_WINDOWS) - 1
POOL_HALO = 16
CONV_WIDTH = 3
CONV_STATE = CONV_WIDTH - 1
CONV_HALO = 8
VMEM_LIMIT = 56 * 1024 * 1024
BF16 = jnp.bfloat16
F32 = jnp.float32
LOG2E = math.log2(math.e)
Q_SCALE = HEAD_DIM ** -0.5 * LOG2E


def _cparams(*sem):
    return pltpu.CompilerParams(dimension_semantics=sem, vmem_limit_bytes=VMEM_LIMIT)


def _const_spec(shape):
    nd = len(shape)
    return pl.BlockSpec(shape, lambda *_: (0,) * nd)


def _rmsnorm(x, g):
    return x * lax.rsqrt(jnp.mean(x * x, axis=-1, keepdims=True) + EPS) * g


def _dot(a, b):
    return jnp.dot(a, b, preferred_element_type=F32)


def _softplus2(z):
    neg_abs = pltpu.bitcast(pltpu.bitcast(z, jnp.int32) | jnp.int32(-2 ** 31), F32)
    return jnp.maximum(z, 0.0) + jnp.log2(1.0 + jnp.exp2(neg_abs))


def _in_proj_kernel(x_ref, g_ref, w_ref, u_ref, q_ref, k_ref, v_ref, *kv_refs, pw, aw):
    h = _rmsnorm(x_ref[...], g_ref[...]).astype(BF16)
    o1, o2, o3, o4 = pw, pw + aw, pw + 2 * aw, pw + 3 * aw
    u_ref[...] = _dot(h, w_ref[:, :o1])
    q_ref[...] = (_dot(h, w_ref[:, o1:o2]) * Q_SCALE).astype(q_ref.dtype)
    k = _dot(h, w_ref[:, o2:o3])
    v = _dot(h, w_ref[:, o3:o4])
    if kv_refs:
        ktb_ref, vb_ref = kv_refs
        kt = k.T
        k_ref[0] = kt
        v_ref[0] = v.T
        ktb_ref[0] = kt.astype(BF16)
        vb_ref[...] = v.astype(BF16)
    else:
        k_ref[...] = k
        v_ref[...] = v


def _in_proj(x2d, norm_g, w_in_bf, *, tm, seq_len, pw, aw, q_dtype):
    n, d = x2d.shape
    inw = w_in_bf.shape[1]
    grid = (n // tm,)
    row = lambda w: pl.BlockSpec((tm, w), lambda i: (i, 0))
    kv_shape, kv_spec = jax.ShapeDtypeStruct((n, aw), F32), row(aw)
    extra_shape, extra_specs = [], []
    if seq_len is not None:
        tps = seq_len // tm
        kv_shape = jax.ShapeDtypeStruct((n // seq_len, aw, seq_len), F32)
        kv_spec = pl.BlockSpec((1, aw, tm), lambda i: (i // tps, 0, i % tps))
        extra_shape = [jax.ShapeDtypeStruct((n // seq_len, aw, seq_len), BF16),
                       jax.ShapeDtypeStruct((n, aw), BF16)]
        extra_specs = [kv_spec, row(aw)]
    out_shape = [jax.ShapeDtypeStruct((n, pw), F32), jax.ShapeDtypeStruct((n, aw), q_dtype),
                 kv_shape, kv_shape] + extra_shape
    out_specs = [row(pw), row(aw), kv_spec, kv_spec] + extra_specs
    return pl.pallas_call(
        functools.partial(_in_proj_kernel, pw=pw, aw=aw),
        grid=grid,
        in_specs=[row(d), _const_spec((1, d)), _const_spec((d, inw))],
        out_specs=out_specs,
        out_shape=out_shape,
        compiler_params=_cparams("parallel"),
        name="in_proj",
    )(x2d, norm_g.reshape(1, d), w_in_bf)


def _sb_tile(z, tri, carry, mask):
    sp = _softplus2(z)
    if mask is not None:
        sp = jnp.where(mask, sp, 0.0)
    right = _dot(sp.astype(BF16), tri)
    w = jnp.exp2(z - sp - right - carry)
    if mask is not None:
        w = jnp.where(mask, w, 0.0)
    return w, jnp.sum(sp, axis=1, keepdims=True)


def _prompt_attn_kernel(bias_ref, q_ref, kt_ref, v_ref, tri_ref, o_ref, acc_ref, carry_ref, *, tq, tk, groups):
    gi = pl.program_id(1)
    i = pl.program_id(2)
    n_heads = groups * HEADS_PER_GROUP
    lane = lax.broadcasted_iota(jnp.int32, (tq, LANES), 1)
    qh = []
    for p in range(groups):
        q2 = q_ref[0, :, p * LANES:(p + 1) * LANES]
        zero = jnp.zeros_like(q2)
        qh += [jnp.where(lane < HEAD_DIM, q2, zero), jnp.where(lane >= HEAD_DIM, q2, zero)]
    bias = [bias_ref[gi * n_heads + h] * LOG2E for h in range(n_heads)]
    tri = tri_ref[...]
    acc_ref[...] = jnp.zeros_like(acc_ref)
    carry_ref[...] = jnp.zeros_like(carry_ref)
    ratio = tq // tk

    def step(j, r0, masked):
        start = pl.multiple_of(j * tk, tk)
        mask = None
        if masked:
            row = lax.broadcasted_iota(jnp.int32, (tq - r0, tk), 0)
            col = lax.broadcasted_iota(jnp.int32, (tq - r0, tk), 1)
            mask = col < row
        for h in range(n_heads):
            p = h // HEADS_PER_GROUP
            kt = kt_ref[0, p * LANES:(p + 1) * LANES, pl.ds(start, tk)]
            vv = v_ref[0, pl.ds(start, tk), p * LANES:(p + 1) * LANES]
            z = _dot(qh[h][r0:], kt) + bias[h]
            w, tot = _sb_tile(z, tri, carry_ref[h, r0:], mask)
            acc_ref[h, r0:] += _dot(w.astype(BF16), vv)
            carry_ref[h, r0:] += tot

    for c in range(ratio - 1, -1, -1):
        step(i * ratio + c, c * tk, True)

    def body(it, carry):
        step(i * ratio - 1 - it, 0, False)
        return carry

    lax.fori_loop(0, i * ratio, body, 0)
    for p in range(groups):
        o_ref[0, :, p * LANES:(p + 1) * LANES] = jnp.where(
            lane < HEAD_DIM, acc_ref[2 * p], acc_ref[2 * p + 1]).astype(o_ref.dtype)


def _strict_tri(n):
    r = lax.broadcasted_iota(jnp.int32, (n, n), 0)
    c = lax.broadcasted_iota(jnp.int32, (n, n), 1)
    return (r > c).astype(BF16)


def _prompt_attention(q_bf, kt_bf, v_bf, sb_bias, *, tq, tk, groups):
    b, t, aw = q_bf.shape
    w = groups * LANES
    return pl.pallas_call(
        functools.partial(_prompt_attn_kernel, tq=tq, tk=tk, groups=groups),
        grid=(b, aw // w, t // tq),
        in_specs=[pl.BlockSpec(memory_space=pltpu.SMEM),
                  pl.BlockSpec((1, tq, w), lambda bi, g, i: (bi, i, g)),
                  pl.BlockSpec((1, w, t), lambda bi, g, i: (bi, g, 0)),
                  pl.BlockSpec((1, t, w), lambda bi, g, i: (bi, 0, g)),
                  _const_spec((tk, tk))],
        out_specs=pl.BlockSpec((1, tq, w), lambda bi, g, i: (bi, i, g)),
        out_shape=jax.ShapeDtypeStruct((b, t, aw), BF16),
        scratch_shapes=[pltpu.VMEM((groups * HEADS_PER_GROUP, tq, LANES), F32),
                        pltpu.VMEM((groups * HEADS_PER_GROUP, tq, 1), F32)],
        compiler_params=_cparams("parallel", "parallel", "parallel"),
        name="prompt_attn",
    )(sb_bias, q_bf, kt_bf, v_bf, _strict_tri(tk))


def _decode_attn_kernel(pt_ref, q_ref, kn_ref, vn_ref, bias_ref, tri_ref, *rest,
                        pages_per_step, page, n_heads, dec_seq, chunk):
    k_refs = rest[:pages_per_step]
    v_refs = rest[pages_per_step:2 * pages_per_step]
    o_ref = rest[2 * pages_per_step]
    qbd_ref, kbuf_ref, vbuf_ref, acc_ref, carry_ref = rest[2 * pages_per_step + 1:]
    j = pl.program_id(1)
    rows = n_heads * dec_seq
    aw = n_heads * HEAD_DIM
    bias = bias_ref[...][:, :1]
    tri = tri_ref[...]

    def tile(kk_t, vv_t, mask):
        n = kk_t.shape[1]
        z = _dot(qbd_ref[...], kk_t) + bias
        w, tot = _sb_tile(z, tri[:n, :n], carry_ref[...], mask)
        acc_ref[...] += lax.dot_general(w.astype(BF16), vv_t, (((1,), (1,)), ((), ())),
                                        preferred_element_type=F32)
        carry_ref[...] += tot

    @pl.when(j == 0)
    def _():
        q = q_ref[0]
        qt = jnp.concatenate([q] * n_heads, axis=0)
        r = lax.broadcasted_iota(jnp.int32, (rows, aw), 0)
        c = lax.broadcasted_iota(jnp.int32, (rows, aw), 1)
        qbd_ref[...] = jnp.where(c // HEAD_DIM == r // dec_seq, qt, 0.0).astype(BF16)
        acc_ref[...] = jnp.zeros_like(acc_ref)
        carry_ref[...] = jnp.zeros_like(carry_ref)
        pad = jnp.zeros((page - dec_seq, aw), F32)
        kn_t = jnp.concatenate([kn_ref[0], pad], axis=0).T.astype(BF16)
        vn_t = jnp.concatenate([vn_ref[0], pad], axis=0).T.astype(BF16)
        rq = lax.broadcasted_iota(jnp.int32, (rows, page), 0) % dec_seq
        ck = lax.broadcasted_iota(jnp.int32, (rows, page), 1)
        tile(kn_t, vn_t, ck < rq)

    for p in range(pages_per_step):
        dst = (pages_per_step - 1 - p) * page
        kbuf_ref[:, dst:dst + page] = k_refs[p][0].astype(BF16)
        vbuf_ref[:, dst:dst + page] = v_refs[p][0].astype(BF16)
    keys = pages_per_step * page
    z = _dot(qbd_ref[...], kbuf_ref[...]) + bias
    sp = _softplus2(z)
    starts = list(range(0, keys, chunk))
    sp_c = [sp[:, c0:c0 + chunk] for c0 in starts]
    right = [_dot(s.astype(BF16), tri) for s in sp_c]
    totals = [jnp.sum(s, axis=1, keepdims=True) for s in sp_c]
    carry = carry_ref[...]
    ws = [None] * len(starts)
    for ci in range(len(starts) - 1, -1, -1):
        c0 = starts[ci]
        ws[ci] = jnp.exp2(z[:, c0:c0 + chunk] - sp_c[ci] - right[ci] - carry).astype(BF16)
        carry = carry + totals[ci]
    acc_ref[...] += lax.dot_general(jnp.concatenate(ws, axis=1), vbuf_ref[...],
                                    (((1,), (1,)), ((), ())), preferred_element_type=F32)
    carry_ref[...] = carry

    @pl.when(j == pl.num_programs(1) - 1)
    def _():
        acc = acc_ref[...]
        r = lax.broadcasted_iota(jnp.int32, (rows, aw), 0)
        c = lax.broadcasted_iota(jnp.int32, (rows, aw), 1)
        sel = jnp.where(c // HEAD_DIM == r // dec_seq, acc, 0.0)
        out = sel[:dec_seq]
        for h in range(1, n_heads):
            out = out + sel[h * dec_seq:(h + 1) * dec_seq]
        o_ref[0] = out.astype(o_ref.dtype)


def _decode_attention(q_s, k_new, v_new, cache_k, cache_v, page_table, sb_bias, *, pages_per_step):
    bs, ts, aw = q_s.shape
    n_phys, _, page = cache_k.shape
    n_pages = page_table.shape[1]
    n_heads = aw // HEAD_DIM
    rows = n_heads * ts
    steps = n_pages // pages_per_step
    chunk = min(2 * page, pages_per_step * page)
    bias_rows = jnp.broadcast_to(jnp.repeat(sb_bias * LOG2E, ts)[:, None], (rows, LANES)).astype(F32)

    def page_spec(p):
        def imap(b, j, pt):
            return (pt[b * n_pages + (n_pages - 1 - (j * pages_per_step + p))], 0, 0)
        return pl.BlockSpec((1, aw, page), imap)

    seq_spec = pl.BlockSpec((1, ts, aw), lambda b, j, pt: (b, 0, 0))
    grid_spec = pltpu.PrefetchScalarGridSpec(
        num_scalar_prefetch=1,
        grid=(bs, steps),
        in_specs=[seq_spec, seq_spec, seq_spec,
                  pl.BlockSpec((rows, LANES), lambda b, j, pt: (0, 0)),
                  pl.BlockSpec((chunk, chunk), lambda b, j, pt: (0, 0))]
                 + [page_spec(p) for p in range(pages_per_step)] * 2,
        out_specs=seq_spec,
        scratch_shapes=[pltpu.VMEM((rows, aw), BF16),
                        pltpu.VMEM((aw, pages_per_step * page), BF16),
                        pltpu.VMEM((aw, pages_per_step * page), BF16),
                        pltpu.VMEM((rows, aw), F32),
                        pltpu.VMEM((rows, 1), F32)],
    )
    return pl.pallas_call(
        functools.partial(_decode_attn_kernel, pages_per_step=pages_per_step, page=page,
                          n_heads=n_heads, dec_seq=ts, chunk=chunk),
        grid_spec=grid_spec,
        out_shape=jax.ShapeDtypeStruct((bs, ts, aw), F32),
        compiler_params=_cparams("parallel", "arbitrary"),
        name="decode_attn",
    )(page_table.reshape(-1), q_s, k_new, v_new, bias_rows, _strict_tri(chunk),
      *([cache_k] * pages_per_step), *([cache_v] * pages_per_step))


def _mix_tail(x, y_pool_groups, yb, nm_ref, wg_ref, pw_ref, ps_ref, wpa_ref, wpb_ref, wout_ref):
    d = x.shape[1]
    h = _rmsnorm(x, nm_ref[...]).astype(BF16)
    ga = jax.nn.sigmoid(_dot(h, wg_ref[:, :d]))
    gb = jax.nn.sigmoid(_dot(h, wg_ref[:, d:]))
    mixed = [_dot(p.astype(BF16), pw_ref[g]) for g, p in enumerate(y_pool_groups)]
    y_a = jnp.concatenate(mixed, axis=1) * ps_ref[...]
    m = ga * _dot(y_a.astype(BF16), wpa_ref[...]) + gb * _dot(yb.astype(BF16), wpb_ref[...])
    return x + _dot(m.astype(BF16), wout_ref[...])


def _mix_prompt_kernel(x_ref, u_ref, yb_ref, nm_ref, wg_ref, pw_ref, ps_ref, wpa_ref, wpb_ref,
                       wout_ref, o_ref, st_ref, halo_ref, *, tps, tm):
    i = pl.program_id(0)
    first = (i % tps) == 0

    @pl.when(first)
    def _():
        halo_ref[...] = jnp.zeros_like(halo_ref)

    u = u_ref[...]
    ext = jnp.concatenate([halo_ref[...], u], axis=0)
    pos = (i % tps) * tm + lax.broadcasted_iota(jnp.int32, (tm, LANES), 0)
    groups = []
    for g, w in enumerate(POOL_WINDOWS):
        e = ext[:, g * LANES:(g + 1) * LANES]
        s, span = e, 1
        while span < w:
            s = s + pltpu.roll(s, span, 0)
            span *= 2
        cnt = jnp.minimum(pos + 1, w).astype(F32)
        groups.append(s[POOL_HALO:] / cnt - e[POOL_HALO:])
    o_ref[...] = _mix_tail(x_ref[...], groups, yb_ref[...], nm_ref, wg_ref,
                           pw_ref, ps_ref, wpa_ref, wpb_ref, wout_ref)
    tail = u[tm - POOL_HALO:]
    halo_ref[...] = tail
    st_ref[0] = tail


def _mix_prompt(x2d, u, yb, norm_g, w_gates, pool_w_bf, pool_scale, wpa, wpb, wout, *, tm, seq_len):
    n, d = x2d.shape
    pw = u.shape[1]
    aw = yb.shape[1]
    tps = seq_len // tm
    row = lambda w: pl.BlockSpec((tm, w), lambda i: (i, 0))
    return pl.pallas_call(
        functools.partial(_mix_prompt_kernel, tps=tps, tm=tm),
        grid=(n // tm,),
        in_specs=[row(d), row(pw), row(aw), _const_spec((1, d)), _const_spec(w_gates.shape),
                  _const_spec(pool_w_bf.shape), _const_spec((1, pw)),
                  _const_spec(wpa.shape), _const_spec(wpb.shape), _const_spec(wout.shape)],
        out_specs=[row(d), pl.BlockSpec((1, POOL_HALO, pw), lambda i: (i // tps, 0, 0))],
        out_shape=[jax.ShapeDtypeStruct((n, d), F32),
                   jax.ShapeDtypeStruct((n // seq_len, POOL_HALO, pw), F32)],
        scratch_shapes=[pltpu.VMEM((POOL_HALO, pw), F32)],
        compiler_params=_cparams("arbitrary"),
        name="mix_prompt",
    )(x2d, u, yb, norm_g.reshape(1, d), w_gates, pool_w_bf, pool_scale.reshape(1, pw), wpa, wpb, wout)


def _mix_sample_kernel(x_ref, st_ref, u_ref, yb_ref, nm_ref, wg_ref, pw_ref, ps_ref, wpa_ref, wpb_ref,
                       wout_ref, o_ref, *, bs, ts, past_len):
    slabs = [st_ref[j * bs:(j + 1) * bs, :] for j in range(POOL_STATE)]
    slabs += [u_ref[t * bs:(t + 1) * bs, :] for t in range(ts)]
    groups = []
    for g, w in enumerate(POOL_WINDOWS):
        e = [s[:, g * LANES:(g + 1) * LANES] for s in slabs]
        s, span = e, 1
        while span < w:
            s = [s[j] + s[j - span] if j >= span else s[j] for j in range(len(s))]
            span *= 2
        rows = []
        for t in range(ts):
            cnt = float(min(past_len + t + 1, w))
            rows.append(s[POOL_STATE + t] / cnt - e[POOL_STATE + t])
        groups.append(jnp.concatenate(rows, axis=0))
    o_ref[...] = _mix_tail(x_ref[...], groups, yb_ref[...], nm_ref, wg_ref,
                           pw_ref, ps_ref, wpa_ref, wpb_ref, wout_ref)


def _mix_sample(x_tm, state_tm, u, yb, norm_g, w_gates, pool_w_bf, pool_scale, wpa, wpb, wout, *,
                bs, ts, past_len):
    n, d = x_tm.shape
    pw = u.shape[1]
    args = (x_tm, state_tm, u, yb, norm_g.reshape(1, d), w_gates, pool_w_bf, pool_scale.reshape(1, pw),
            wpa, wpb, wout)
    return pl.pallas_call(
        functools.partial(_mix_sample_kernel, bs=bs, ts=ts, past_len=past_len),
        grid=(1,),
        in_specs=[_const_spec(a.shape) for a in args],
        out_specs=_const_spec((n, d)),
        out_shape=jax.ShapeDtypeStruct((n, d), F32),
        compiler_params=_cparams("arbitrary"),
        name="mix_sample",
    )(*args)


def _gelu_tanh(x):
    return 0.5 * x * (1.0 + jnp.tanh(0.7978845608028654 * (x + 0.044715 * (x * x * x))))


def _ffn_tail(x, a0, a1, a2, gate, cw_ref, cb_ref, wd_ref, nf_ref):
    conv = cb_ref[...] + a0 * cw_ref[0:1, :] + a1 * cw_ref[1:2, :] + a2 * cw_ref[2:3, :]
    hid = (_gelu_tanh(conv) * gate).astype(BF16)
    x2 = x + _dot(hid, wd_ref[...])
    return _rmsnorm(x2, nf_ref[...])


def _ffn_prompt_kernel(x_ref, g_ref, wa_ref, wb_ref, cw_ref, cb_ref, wd_ref, nf_ref, o_ref, st_ref,
                       halo_ref, *, tps, tm):
    i = pl.program_id(0)

    @pl.when((i % tps) == 0)
    def _():
        halo_ref[...] = jnp.zeros_like(halo_ref)

    x = x_ref[...]
    h = _rmsnorm(x, g_ref[...]).astype(BF16)
    a = _dot(h, wa_ref[...])
    gate = _dot(h, wb_ref[...])
    ext = jnp.concatenate([halo_ref[...], a], axis=0)
    a1 = pltpu.roll(ext, 1, 0)[CONV_HALO:]
    a0 = pltpu.roll(ext, 2, 0)[CONV_HALO:]
    o_ref[...] = _ffn_tail(x, a0, a1, a, gate, cw_ref, cb_ref, wd_ref, nf_ref)
    tail = a[tm - CONV_HALO:]
    halo_ref[...] = tail
    st_ref[0] = tail


def _ffn_prompt(x2d, norm_g, wa, wb, conv_w, conv_b, wd, norm_final, *, tm, seq_len):
    n, d = x2d.shape
    f = wa.shape[1]
    tps = seq_len // tm
    row = pl.BlockSpec((tm, d), lambda i: (i, 0))
    return pl.pallas_call(
        functools.partial(_ffn_prompt_kernel, tps=tps, tm=tm),
        grid=(n // tm,),
        in_specs=[row, _const_spec((1, d)), _const_spec(wa.shape), _const_spec(wb.shape),
                  _const_spec(conv_w.shape), _const_spec((1, f)), _const_spec(wd.shape),
                  _const_spec((1, d))],
        out_specs=[row, pl.BlockSpec((1, CONV_HALO, f), lambda i: (i // tps, 0, 0))],
        out_shape=[jax.ShapeDtypeStruct((n, d), F32),
                   jax.ShapeDtypeStruct((n // seq_len, CONV_HALO, f), F32)],
        scratch_shapes=[pltpu.VMEM((CONV_HALO, f), F32)],
        compiler_params=_cparams("arbitrary"),
        name="ffn_prompt",
    )(x2d, norm_g.reshape(1, d), wa, wb, conv_w, conv_b.reshape(1, f), wd, norm_final.reshape(1, d))


def _ffn_sample_kernel(x_ref, st_ref, g_ref, wa_ref, wb_ref, cw_ref, cb_ref, wd_ref, nf_ref,
                       o_ref, a_ref, *, bs):
    x = x_ref[...]
    n = x.shape[0]
    h = _rmsnorm(x, g_ref[...]).astype(BF16)
    a = _dot(h, wa_ref[...])
    gate = _dot(h, wb_ref[...])
    ext = jnp.concatenate([st_ref[...], a], axis=0)
    a0 = ext[:n]
    a1 = ext[bs:bs + n]
    o_ref[...] = _ffn_tail(x, a0, a1, a, gate, cw_ref, cb_ref, wd_ref, nf_ref)
    a_ref[...] = ext[n:]


def _ffn_sample(x_tm, state_tm, norm_g, wa, wb, conv_w, conv_b, wd, norm_final, *, bs):
    n, d = x_tm.shape
    f = wa.shape[1]
    args = (x_tm, state_tm, norm_g.reshape(1, d), wa, wb, conv_w, conv_b.reshape(1, f), wd,
            norm_final.reshape(1, d))
    return pl.pallas_call(
        functools.partial(_ffn_sample_kernel, bs=bs),
        grid=(1,),
        in_specs=[_const_spec(a.shape) for a in args],
        out_specs=[_const_spec((n, d)), _const_spec((CONV_STATE * bs, f))],
        out_shape=[jax.ShapeDtypeStruct((n, d), F32), jax.ShapeDtypeStruct((CONV_STATE * bs, f), F32)],
        compiler_params=_cparams("arbitrary"),
        name="ffn_sample",
    )(*args)


def _tile(n, pref):
    t = min(pref, n)
    assert n % t == 0, (n, t)
    return t


def kernel(x_prompt, x_sample, cache_k, cache_v, page_table, state_pool, state_conv, norm_mix, w_in, sb_bias, pool_w, pool_scale, w_proj_a, w_proj_b, w_out, norm_ffn, w_up_a, w_up_b, conv_w, conv_b, w_down, norm_final):
    depth = norm_mix.shape[0]
    bp, t, d = x_prompt.shape
    bs, ts, _ = x_sample.shape
    n_heads, hd = cache_k.shape[-2:]
    assert hd == HEAD_DIM
    aw = n_heads * hd
    pw = pool_scale.shape[-1]
    n_phys, page = cache_k.shape[1:3]
    past_len = page_table.shape[1] * page
    assert pw == LANES * len(POOL_WINDOWS) and state_pool.shape[2] == POOL_STATE
    assert state_conv.shape[2] == CONV_STATE and conv_w.shape[1] == CONV_WIDTH

    tm = _tile(t, 256)
    tk = _tile(t, 256)
    tq = _tile(t, 512)
    pps = _tile(page_table.shape[1], 16)

    def pages_keys_last(cache):
        return jnp.transpose(cache, (0, 2, 3, 1)).reshape(n_phys, aw, page)

    assert depth == 1, "the final norm is fused into the (single) layer's FFN kernel"
    l = 0
    kv_cols = pw + 3 * aw
    w_in_bf = w_in[l].astype(BF16)
    w_qkv, w_gates = w_in_bf[:, :kv_cols], w_in_bf[:, kv_cols:]
    pool_w_bf = pool_w[l].astype(BF16)
    wpa, wpb, wout = w_proj_a[l].astype(BF16), w_proj_b[l].astype(BF16), w_out[l].astype(BF16)
    wa, wb, wd = w_up_a[l].astype(BF16), w_up_b[l].astype(BF16), w_down[l].astype(BF16)

    hp = x_prompt.reshape(bp * t, d)
    u, q, kt, vt, kt_bf, v_bf = _in_proj(hp, norm_mix[l], w_qkv, tm=tm, seq_len=t, pw=pw, aw=aw,
                                         q_dtype=BF16)
    yb = _prompt_attention(q.reshape(bp, t, aw), kt_bf, v_bf.reshape(bp, t, aw), sb_bias[l],
                           tq=tq, tk=tk, groups=aw // LANES)
    hp, pool_tail = _mix_prompt(hp, u, yb.reshape(bp * t, aw), norm_mix[l], w_gates, pool_w_bf,
                                pool_scale[l], wpa, wpb, wout, tm=tm, seq_len=t)
    y_prompt, conv_tail = _ffn_prompt(hp, norm_ffn[l], wa, wb, conv_w[l], conv_b[l], wd, norm_final,
                                      tm=tm, seq_len=t)
    rows_major = lambda a: jnp.transpose(a.reshape(bp, n_heads, hd, t), (0, 3, 1, 2))
    k_prompt, v_prompt = rows_major(kt), rows_major(vt)
    pool_prompt = pool_tail[:, POOL_HALO - POOL_STATE:]
    conv_prompt = conv_tail[:, CONV_HALO - CONV_STATE:]

    hs = jnp.transpose(x_sample, (1, 0, 2)).reshape(ts * bs, d)
    us, qs, ksn, vsn = _in_proj(hs, norm_mix[l], w_qkv, tm=_tile(ts * bs, 256), seq_len=None,
                                pw=pw, aw=aw, q_dtype=F32)
    to_bm = lambda a: jnp.transpose(a.reshape(ts, bs, -1), (1, 0, 2))
    ks_bm, vs_bm = to_bm(ksn), to_bm(vsn)
    ybs = _decode_attention(to_bm(qs), ks_bm, vs_bm, pages_keys_last(cache_k[l]),
                            pages_keys_last(cache_v[l]), page_table, sb_bias[l], pages_per_step=pps)
    ybs_tm = jnp.transpose(ybs, (1, 0, 2)).reshape(ts * bs, aw)
    pool_tm = jnp.transpose(state_pool[l], (1, 0, 2)).reshape(POOL_STATE * bs, pw)
    hs = _mix_sample(hs, pool_tm, us, ybs_tm, norm_mix[l], w_gates, pool_w_bf, pool_scale[l],
                     wpa, wpb, wout, bs=bs, ts=ts, past_len=past_len)
    conv_tm = jnp.transpose(state_conv[l], (1, 0, 2)).reshape(CONV_STATE * bs, -1)
    hs_out, a_tail = _ffn_sample(hs, conv_tm, norm_ffn[l], wa, wb, conv_w[l], conv_b[l], wd,
                                 norm_final, bs=bs)
    y_sample = jnp.transpose(hs_out.reshape(ts, bs, d), (1, 0, 2))
    k_sample = ks_bm.reshape(bs, ts, n_heads, hd)
    v_sample = vs_bm.reshape(bs, ts, n_heads, hd)
    pool_sample = jnp.concatenate([state_pool[l], to_bm(us)], axis=1)[:, -POOL_STATE:]
    conv_sample = jnp.transpose(a_tail.reshape(CONV_STATE, bs, -1), (1, 0, 2))

    layer = lambda a: a[None]
    return (y_prompt.reshape(bp, t, d), y_sample, layer(k_prompt), layer(v_prompt),
            layer(pool_prompt), layer(conv_prompt), layer(k_sample), layer(v_sample),
            layer(pool_sample), layer(conv_sample))
```

```python
import functools
import math

import jax
import jax.numpy as jnp
from jax import lax
from jax.experimental import pallas as pl
from jax.experimental.pallas import tpu as pltpu

EPS = 1e-6
HEAD_DIM = 64
LANES = 128
HEADS_PER_GROUP = LANES // HEAD_DIM
POOL_WINDOWS = (2, 4, 8, 16)
POOL_STATE = max(POOL_WINDOWS) - 1
POOL_HALO = 16
CONV_WIDTH = 3
CONV_STATE = CONV_WIDTH - 1
CONV_HALO = 8
VMEM_LIMIT = 56 * 1024 * 1024
BF16 = jnp.bfloat16
F32 = jnp.float32
LOG2E = math.log2(math.e)
Q_SCALE = HEAD_DIM ** -0.5 * LOG2E


def _cparams(*sem):
    return pltpu.CompilerParams(dimension_semantics=sem, vmem_limit_bytes=VMEM_LIMIT)


def _const_spec(shape):
    nd = len(shape)
    return pl.BlockSpec(shape, lambda *_: (0,) * nd)


def _resident_spec(shape):
    nd = len(shape)
    return pl.BlockSpec(shape, lambda *_: (0,) * nd, pipeline_mode=pl.Buffered(1))


def _rmsnorm(x, g):
    return x * lax.rsqrt(jnp.mean(x * x, axis=-1, keepdims=True) + EPS) * g


def _dot(a, b):
    return jnp.dot(a, b, preferred_element_type=F32)


def _softplus2(z):
    neg_abs = pltpu.bitcast(pltpu.bitcast(z, jnp.int32) | jnp.int32(-2 ** 31), F32)
    return jnp.maximum(z, 0.0) + jnp.log2(1.0 + jnp.exp2(neg_abs))


def _in_proj_kernel(x_ref, g_ref, w_ref, u_ref, q_ref, k_ref, v_ref, *kv_refs, pw, aw):
    h = _rmsnorm(x_ref[...], g_ref[...]).astype(BF16)
    o1, o2, o3, o4 = pw, pw + aw, pw + 2 * aw, pw + 3 * aw
    u_ref[...] = _dot(h, w_ref[:, :o1])
    q_ref[...] = (_dot(h, w_ref[:, o1:o2]) * Q_SCALE).astype(q_ref.dtype)
    k = _dot(h, w_ref[:, o2:o3])
    v = _dot(h, w_ref[:, o3:o4])
    if kv_refs:
        ktb_ref, vb_ref = kv_refs
        kt = k.T
        k_ref[0] = kt
        v_ref[0] = v.T
        ktb_ref[0] = kt.astype(BF16)
        vb_ref[...] = v.astype(BF16)
    else:
        k_ref[...] = k
        v_ref[...] = v


def _in_proj(x2d, norm_g, w_in_bf, *, tm, seq_len, pw, aw, q_dtype):
    n, d = x2d.shape
    inw = w_in_bf.shape[1]
    grid = (n // tm,)
    row = lambda w: pl.BlockSpec((tm, w), lambda i: (i, 0))
    kv_shape, kv_spec = jax.ShapeDtypeStruct((n, aw), F32), row(aw)
    extra_shape, extra_specs = [], []
    if seq_len is not None:
        tps = seq_len // tm
        kv_shape = jax.ShapeDtypeStruct((n // seq_len, aw, seq_len), F32)
        kv_spec = pl.BlockSpec((1, aw, tm), lambda i: (i // tps, 0, i % tps))
        extra_shape = [jax.ShapeDtypeStruct((n // seq_len, aw, seq_len), BF16),
                       jax.ShapeDtypeStruct((n, aw), BF16)]
        extra_specs = [kv_spec, row(aw)]
    out_shape = [jax.ShapeDtypeStruct((n, pw), F32), jax.ShapeDtypeStruct((n, aw), q_dtype),
                 kv_shape, kv_shape] + extra_shape
    out_specs = [row(pw), row(aw), kv_spec, kv_spec] + extra_specs
    return pl.pallas_call(
        functools.partial(_in_proj_kernel, pw=pw, aw=aw),
        grid=grid,
        in_specs=[row(d), _resident_spec((1, d)), _resident_spec((d, inw))],
        out_specs=out_specs,
        out_shape=out_shape,
        compiler_params=_cparams("parallel"),
        name="in_proj",
    )(x2d, norm_g.reshape(1, d), w_in_bf)


def _sb_tile(z, tri, carry, mask):
    sp = _softplus2(z)
    if mask is not None:
        sp = jnp.where(mask, sp, 0.0)
    right = _dot(sp.astype(BF16), tri)
    w = jnp.exp2(z - sp - right - carry)
    if mask is not None:
        w = jnp.where(mask, w, 0.0)
    return w, jnp.sum(sp, axis=1, keepdims=True)


def _prompt_attn_kernel(bias_ref, q_ref, kt_ref, v_ref, tri_ref, o_ref, acc_ref, carry_ref, *, tq, tk, groups):
    gi = pl.program_id(1)
    i = pl.program_id(2)
    n_heads = groups * HEADS_PER_GROUP
    lane = lax.broadcasted_iota(jnp.int32, (tq, LANES), 1)
    qh = []
    for p in range(groups):
        q2 = q_ref[0, :, p * LANES:(p + 1) * LANES]
        zero = jnp.zeros_like(q2)
        qh += [jnp.where(lane < HEAD_DIM, q2, zero), jnp.where(lane >= HEAD_DIM, q2, zero)]
    bias = [bias_ref[gi * n_heads + h] * LOG2E for h in range(n_heads)]
    tri = tri_ref[...]
    acc_ref[...] = jnp.zeros_like(acc_ref)
    carry_ref[...] = jnp.zeros_like(carry_ref)
    ratio = tq // tk

    def step(j, r0, masked):
        start = pl.multiple_of(j * tk, tk)
        mask = None
        if masked:
            row = lax.broadcasted_iota(jnp.int32, (tq - r0, tk), 0)
            col = lax.broadcasted_iota(jnp.int32, (tq - r0, tk), 1)
            mask = col < row
        for h in range(n_heads):
            p = h // HEADS_PER_GROUP
            kt = kt_ref[0, p * LANES:(p + 1) * LANES, pl.ds(start, tk)]
            vv = v_ref[0, pl.ds(start, tk), p * LANES:(p + 1) * LANES]
            z = _dot(qh[h][r0:], kt) + bias[h]
            w, tot = _sb_tile(z, tri, carry_ref[h, r0:], mask)
            acc_ref[h, r0:] += _dot(w.astype(BF16), vv)
            carry_ref[h, r0:] += tot

    for c in range(ratio - 1, -1, -1):
        step(i * ratio + c, c * tk, True)

    def body(it, carry):
        step(i * ratio - 1 - it, 0, False)
        return carry

    lax.fori_loop(0, i * ratio, body, 0)
    for p in range(groups):
        o_ref[0, :, p * LANES:(p + 1) * LANES] = jnp.where(
            lane < HEAD_DIM, acc_ref[2 * p], acc_ref[2 * p + 1]).astype(o_ref.dtype)


def _strict_tri(n):
    r = lax.broadcasted_iota(jnp.int32, (n, n), 0)
    c = lax.broadcasted_iota(jnp.int32, (n, n), 1)
    return (r > c).astype(BF16)


def _prompt_attention(q_bf, kt_bf, v_bf, sb_bias, *, tq, tk, groups):
    b, t, aw = q_bf.shape
    w = groups * LANES
    return pl.pallas_call(
        functools.partial(_prompt_attn_kernel, tq=tq, tk=tk, groups=groups),
        grid=(b, aw // w, t // tq),
        in_specs=[pl.BlockSpec(memory_space=pltpu.SMEM),
                  pl.BlockSpec((1, tq, w), lambda bi, g, i: (bi, i, g)),
                  pl.BlockSpec((1, w, t), lambda bi, g, i: (bi, g, 0)),
                  pl.BlockSpec((1, t, w), lambda bi, g, i: (bi, 0, g)),
                  _const_spec((tk, tk))],
        out_specs=pl.BlockSpec((1, tq, w), lambda bi, g, i: (bi, i, g)),
        out_shape=jax.ShapeDtypeStruct((b, t, aw), BF16),
        scratch_shapes=[pltpu.VMEM((groups * HEADS_PER_GROUP, tq, LANES), F32),
                        pltpu.VMEM((groups * HEADS_PER_GROUP, tq, 1), F32)],
        compiler_params=_cparams("parallel", "parallel", "parallel"),
        name="prompt_attn",
    )(sb_bias, q_bf, kt_bf, v_bf, _strict_tri(tk))


def _decode_attn_kernel(pt_ref, q_ref, kn_ref, vn_ref, bias_ref, tri_ref, *rest,
                        pages_per_step, page, n_heads, dec_seq, chunk):
    k_refs = rest[:pages_per_step]
    v_refs = rest[pages_per_step:2 * pages_per_step]
    o_ref = rest[2 * pages_per_step]
    qbd_ref, kbuf_ref, vbuf_ref, acc_ref, carry_ref = rest[2 * pages_per_step + 1:]
    j = pl.program_id(1)
    rows = n_heads * dec_seq
    aw = n_heads * HEAD_DIM
    bias = bias_ref[...][:, :1]
    tri = tri_ref[...]

    def tile(kk_t, vv_t, mask):
        n = kk_t.shape[1]
        z = _dot(qbd_ref[...], kk_t) + bias
        w, tot = _sb_tile(z, tri[:n, :n], carry_ref[...], mask)
        acc_ref[...] += lax.dot_general(w.astype(BF16), vv_t, (((1,), (1,)), ((), ())),
                                        preferred_element_type=F32)
        carry_ref[...] += tot

    @pl.when(j == 0)
    def _():
        q = q_ref[0]
        qt = jnp.concatenate([q] * n_heads, axis=0)
        r = lax.broadcasted_iota(jnp.int32, (rows, aw), 0)
        c = lax.broadcasted_iota(jnp.int32, (rows, aw), 1)
        qbd_ref[...] = jnp.where(c // HEAD_DIM == r // dec_seq, qt, 0.0).astype(BF16)
        acc_ref[...] = jnp.zeros_like(acc_ref)
        carry_ref[...] = jnp.zeros_like(carry_ref)
        pad = jnp.zeros((page - dec_seq, aw), F32)
        kn_t = jnp.concatenate([kn_ref[0], pad], axis=0).T.astype(BF16)
        vn_t = jnp.concatenate([vn_ref[0], pad], axis=0).T.astype(BF16)
        rq = lax.broadcasted_iota(jnp.int32, (rows, page), 0) % dec_seq
        ck = lax.broadcasted_iota(jnp.int32, (rows, page), 1)
        tile(kn_t, vn_t, ck < rq)

    for p in range(pages_per_step):
        dst = (pages_per_step - 1 - p) * page
        kbuf_ref[:, dst:dst + page] = k_refs[p][0].astype(BF16)
        vbuf_ref[:, dst:dst + page] = v_refs[p][0].astype(BF16)
    keys = pages_per_step * page
    z = _dot(qbd_ref[...], kbuf_ref[...]) + bias
    sp = _softplus2(z)
    starts = list(range(0, keys, chunk))
    sp_c = [sp[:, c0:c0 + chunk] for c0 in starts]
    right = [_dot(s.astype(BF16), tri) for s in sp_c]
    totals = [jnp.sum(s, axis=1, keepdims=True) for s in sp_c]
    carry = carry_ref[...]
    ws = [None] * len(starts)
    for ci in range(len(starts) - 1, -1, -1):
        c0 = starts[ci]
        ws[ci] = jnp.exp2(z[:, c0:c0 + chunk] - sp_c[ci] - right[ci] - carry).astype(BF16)
        carry = carry + totals[ci]
    acc_ref[...] += lax.dot_general(jnp.concatenate(ws, axis=1), vbuf_ref[...],
                                    (((1,), (1,)), ((), ())), preferred_element_type=F32)
    carry_ref[...] = carry

    @pl.when(j == pl.num_programs(1) - 1)
    def _():
        acc = acc_ref[...]
        r = lax.broadcasted_iota(jnp.int32, (rows, aw), 0)
        c = lax.broadcasted_iota(jnp.int32, (rows, aw), 1)
        sel = jnp.where(c // HEAD_DIM == r // dec_seq, acc, 0.0)
        out = sel[:dec_seq]
        for h in range(1, n_heads):
            out = out + sel[h * dec_seq:(h + 1) * dec_seq]
        o_ref[0] = out.astype(o_ref.dtype)


def _decode_attention(q_s, k_new, v_new, cache_k, cache_v, page_table, sb_bias, *, pages_per_step):
    bs, ts, aw = q_s.shape
    n_phys, _, page = cache_k.shape
    n_pages = page_table.shape[1]
    n_heads = aw // HEAD_DIM
    rows = n_heads * ts
    steps = n_pages // pages_per_step
    chunk = min(2 * page, pages_per_step * page)
    bias_rows = jnp.broadcast_to(jnp.repeat(sb_bias * LOG2E, ts)[:, None], (rows, LANES)).astype(F32)

    def page_spec(p):
        def imap(b, j, pt):
            return (pt[b * n_pages + (n_pages - 1 - (j * pages_per_step + p))], 0, 0)
        return pl.BlockSpec((1, aw, page), imap)

    seq_spec = pl.BlockSpec((1, ts, aw), lambda b, j, pt: (b, 0, 0))
    grid_spec = pltpu.PrefetchScalarGridSpec(
        num_scalar_prefetch=1,
        grid=(bs, steps),
        in_specs=[seq_spec, seq_spec, seq_spec,
                  pl.BlockSpec((rows, LANES), lambda b, j, pt: (0, 0)),
                  pl.BlockSpec((chunk, chunk), lambda b, j, pt: (0, 0))]
                 + [page_spec(p) for p in range(pages_per_step)] * 2,
        out_specs=seq_spec,
        scratch_shapes=[pltpu.VMEM((rows, aw), BF16),
                        pltpu.VMEM((aw, pages_per_step * page), BF16),
                        pltpu.VMEM((aw, pages_per_step * page), BF16),
                        pltpu.VMEM((rows, aw), F32),
                        pltpu.VMEM((rows, 1), F32)],
    )
    return pl.pallas_call(
        functools.partial(_decode_attn_kernel, pages_per_step=pages_per_step, page=page,
                          n_heads=n_heads, dec_seq=ts, chunk=chunk),
        grid_spec=grid_spec,
        out_shape=jax.ShapeDtypeStruct((bs, ts, aw), F32),
        compiler_params=_cparams("parallel", "arbitrary"),
        name="decode_attn",
    )(page_table.reshape(-1), q_s, k_new, v_new, bias_rows, _strict_tri(chunk),
      *([cache_k] * pages_per_step), *([cache_v] * pages_per_step))


def _mix_tail(x, y_pool_groups, yb, nm_ref, wg_ref, pw_ref, ps_ref, wpa_ref, wpb_ref, wout_ref):
    d = x.shape[1]
    h = _rmsnorm(x, nm_ref[...]).astype(BF16)
    ga = jax.nn.sigmoid(_dot(h, wg_ref[:, :d]))
    gb = jax.nn.sigmoid(_dot(h, wg_ref[:, d:]))
    mixed = [_dot(p.astype(BF16), pw_ref[g]) for g, p in enumerate(y_pool_groups)]
    y_a = jnp.concatenate(mixed, axis=1) * ps_ref[...]
    m = ga * _dot(y_a.astype(BF16), wpa_ref[...]) + gb * _dot(yb.astype(BF16), wpb_ref[...])
    return x + _dot(m.astype(BF16), wout_ref[...])


def _mix_prompt_kernel(x_ref, u_ref, yb_ref, nm_ref, wg_ref, pw_ref, ps_ref, wpa_ref, wpb_ref,
                       wout_ref, o_ref, st_ref, halo_ref, *, tps, tm):
    i = pl.program_id(0)
    first = (i % tps) == 0

    @pl.when(first)
    def _():
        halo_ref[...] = jnp.zeros_like(halo_ref)

    u = u_ref[...]
    ext = jnp.concatenate([halo_ref[...], u], axis=0)
    pos = (i % tps) * tm + lax.broadcasted_iota(jnp.int32, (tm, LANES), 0)
    groups = []
    for g, w in enumerate(POOL_WINDOWS):
        e = ext[:, g * LANES:(g + 1) * LANES]
        s, span = e, 1
        while span < w:
            s = s + pltpu.roll(s, span, 0)
            span *= 2
        cnt = jnp.minimum(pos + 1, w).astype(F32)
        groups.append(s[POOL_HALO:] / cnt - e[POOL_HALO:])
    o_ref[...] = _mix_tail(x_ref[...], groups, yb_ref[...], nm_ref, wg_ref,
                           pw_ref, ps_ref, wpa_ref, wpb_ref, wout_ref)
    tail = u[tm - POOL_HALO:]
    halo_ref[...] = tail
    st_ref[0] = tail


def _mix_prompt(x2d, u, yb, norm_g, w_gates, pool_w_bf, pool_scale, wpa, wpb, wout, *, tm, seq_len):
    n, d = x2d.shape
    pw = u.shape[1]
    aw = yb.shape[1]
    tps = seq_len // tm
    row = lambda w: pl.BlockSpec((tm, w), lambda i: (i, 0))
    return pl.pallas_call(
        functools.partial(_mix_prompt_kernel, tps=tps, tm=tm),
        grid=(n // tm,),
        in_specs=[row(d), row(pw), row(aw), _resident_spec((1, d)), _resident_spec(w_gates.shape),
                  _resident_spec(pool_w_bf.shape), _resident_spec((1, pw)),
                  _resident_spec(wpa.shape), _resident_spec(wpb.shape), _resident_spec(wout.shape)],
        out_specs=[row(d), pl.BlockSpec((1, POOL_HALO, pw), lambda i: (i // tps, 0, 0))],
        out_shape=[jax.ShapeDtypeStruct((n, d), F32),
                   jax.ShapeDtypeStruct((n // seq_len, POOL_HALO, pw), F32)],
        scratch_shapes=[pltpu.VMEM((POOL_HALO, pw), F32)],
        compiler_params=_cparams("arbitrary"),
        name="mix_prompt",
    )(x2d, u, yb, norm_g.reshape(1, d), w_gates, pool_w_bf, pool_scale.reshape(1, pw), wpa, wpb, wout)


def _mix_sample_kernel(x_ref, st_ref, u_ref, yb_ref, nm_ref, wg_ref, pw_ref, ps_ref, wpa_ref, wpb_ref,
                       wout_ref, o_ref, *, bs, ts, past_len):
    slabs = [st_ref[j * bs:(j + 1) * bs, :] for j in range(POOL_STATE)]
    slabs += [u_ref[t * bs:(t + 1) * bs, :] for t in range(ts)]
    groups = []
    for g, w in enumerate(POOL_WINDOWS):
        e = [s[:, g * LANES:(g + 1) * LANES] for s in slabs]
        s, span = e, 1
        while span < w:
            s = [s[j] + s[j - span] if j >= span else s[j] for j in range(len(s))]
            span *= 2
        rows = []
        for t in range(ts):
            cnt = float(min(past_len + t + 1, w))
            rows.append(s[POOL_STATE + t] / cnt - e[POOL_STATE + t])
        groups.append(jnp.concatenate(rows, axis=0))
    o_ref[...] = _mix_tail(x_ref[...], groups, yb_ref[...], nm_ref, wg_ref,
                           pw_ref, ps_ref, wpa_ref, wpb_ref, wout_ref)


def _mix_sample(x_tm, state_tm, u, yb, norm_g, w_gates, pool_w_bf, pool_scale, wpa, wpb, wout, *,
                bs, ts, past_len):
    n, d = x_tm.shape
    pw = u.shape[1]
    args = (x_tm, state_tm, u, yb, norm_g.reshape(1, d), w_gates, pool_w_bf, pool_scale.reshape(1, pw),
            wpa, wpb, wout)
    return pl.pallas_call(
        functools.partial(_mix_sample_kernel, bs=bs, ts=ts, past_len=past_len),
        grid=(1,),
        in_specs=[_const_spec(a.shape) for a in args],
        out_specs=_const_spec((n, d)),
        out_shape=jax.ShapeDtypeStruct((n, d), F32),
        compiler_params=_cparams("arbitrary"),
        name="mix_sample",
    )(*args)


def _gelu_tanh(x):
    return 0.5 * x * (1.0 + jnp.tanh(0.7978845608028654 * (x + 0.044715 * (x * x * x))))


def _ffn_tail(x, a0, a1, a2, gate, cw_ref, cb_ref, wd_ref, nf_ref):
    conv = cb_ref[...] + a0 * cw_ref[0:1, :] + a1 * cw_ref[1:2, :] + a2 * cw_ref[2:3, :]
    hid = (_gelu_tanh(conv) * gate).astype(BF16)
    x2 = x + _dot(hid, wd_ref[...])
    return _rmsnorm(x2, nf_ref[...])


def _ffn_prompt_kernel(x_ref, g_ref, wa_ref, wb_ref, cw_ref, cb_ref, wd_ref, nf_ref, o_ref, st_ref,
                       halo_ref, *, tps, tm):
    i = pl.program_id(0)

    @pl.when((i % tps) == 0)
    def _():
        halo_ref[...] = jnp.zeros_like(halo_ref)

    x = x_ref[...]
    h = _rmsnorm(x, g_ref[...]).astype(BF16)
    a = _dot(h, wa_ref[...])
    gate = _dot(h, wb_ref[...])
    ext = jnp.concatenate([halo_ref[...], a], axis=0)
    a1 = pltpu.roll(ext, 1, 0)[CONV_HALO:]
    a0 = pltpu.roll(ext, 2, 0)[CONV_HALO:]
    o_ref[...] = _ffn_tail(x, a0, a1, a, gate, cw_ref, cb_ref, wd_ref, nf_ref)
    tail = a[tm - CONV_HALO:]
    halo_ref[...] = tail
    st_ref[0] = tail


def _ffn_prompt(x2d, norm_g, wa, wb, conv_w, conv_b, wd, norm_final, *, tm, seq_len):
    n, d = x2d.shape
    f = wa.shape[1]
    tps = seq_len // tm
    row = pl.BlockSpec((tm, d), lambda i: (i, 0))
    return pl.pallas_call(
        functools.partial(_ffn_prompt_kernel, tps=tps, tm=tm),
        grid=(n // tm,),
        in_specs=[row, _resident_spec((1, d)), _resident_spec(wa.shape), _resident_spec(wb.shape),
                  _resident_spec(conv_w.shape), _resident_spec((1, f)), _resident_spec(wd.shape),
                  _resident_spec((1, d))],
        out_specs=[row, pl.BlockSpec((1, CONV_HALO, f), lambda i: (i // tps, 0, 0))],
        out_shape=[jax.ShapeDtypeStruct((n, d), F32),
                   jax.ShapeDtypeStruct((n // seq_len, CONV_HALO, f), F32)],
        scratch_shapes=[pltpu.VMEM((CONV_HALO, f), F32)],
        compiler_params=_cparams("arbitrary"),
        name="ffn_prompt",
    )(x2d, norm_g.reshape(1, d), wa, wb, conv_w, conv_b.reshape(1, f), wd, norm_final.reshape(1, d))


def _ffn_sample_kernel(x_ref, st_ref, g_ref, wa_ref, wb_ref, cw_ref, cb_ref, wd_ref, nf_ref,
                       o_ref, a_ref, *, bs):
    x = x_ref[...]
    n = x.shape[0]
    h = _rmsnorm(x, g_ref[...]).astype(BF16)
    a = _dot(h, wa_ref[...])
    gate = _dot(h, wb_ref[...])
    ext = jnp.concatenate([st_ref[...], a], axis=0)
    a0 = ext[:n]
    a1 = ext[bs:bs + n]
    o_ref[...] = _ffn_tail(x, a0, a1, a, gate, cw_ref, cb_ref, wd_ref, nf_ref)
    a_ref[...] = ext[n:]


def _ffn_sample(x_tm, state_tm, norm_g, wa, wb, conv_w, conv_b, wd, norm_final, *, bs):
    n, d = x_tm.shape
    f = wa.shape[1]
    args = (x_tm, state_tm, norm_g.reshape(1, d), wa, wb, conv_w, conv_b.reshape(1, f), wd,
            norm_final.reshape(1, d))
    return pl.pallas_call(
        functools.partial(_ffn_sample_kernel, bs=bs),
        grid=(1,),
        in_specs=[_const_spec(a.shape) for a in args],
        out_specs=[_const_spec((n, d)), _const_spec((CONV_STATE * bs, f))],
        out_shape=[jax.ShapeDtypeStruct((n, d), F32), jax.ShapeDtypeStruct((CONV_STATE * bs, f), F32)],
        compiler_params=_cparams("arbitrary"),
        name="ffn_sample",
    )(*args)


def _tile(n, pref):
    t = min(pref, n)
    assert n % t == 0, (n, t)
    return t


def kernel(x_prompt, x_sample, cache_k, cache_v, page_table, state_pool, state_conv, norm_mix, w_in, sb_bias, pool_w, pool_scale, w_proj_a, w_proj_b, w_out, norm_ffn, w_up_a, w_up_b, conv_w, conv_b, w_down, norm_final):
    depth = norm_mix.shape[0]
    bp, t, d = x_prompt.shape
    bs, ts, _ = x_sample.shape
    n_heads, hd = cache_k.shape[-2:]
    assert hd == HEAD_DIM
    aw = n_heads * hd
    pw = pool_scale.shape[-1]
    n_phys, page = cache_k.shape[1:3]
    past_len = page_table.shape[1] * page
    assert pw == LANES * len(POOL_WINDOWS) and state_pool.shape[2] == POOL_STATE
    assert state_conv.shape[2] == CONV_STATE and conv_w.shape[1] == CONV_WIDTH

    tm = _tile(t, 512)
    tk = _tile(t, 256)
    tq = _tile(t, 512)
    pps = _tile(page_table.shape[1], 32)

    def pages_keys_last(cache):
        return jnp.transpose(cache, (0, 2, 3, 1)).reshape(n_phys, aw, page)

    assert depth == 1, "the final norm is fused into the (single) layer's FFN kernel"
    l = 0
    kv_cols = pw + 3 * aw
    w_in_bf = w_in[l].astype(BF16)
    w_qkv, w_gates = w_in_bf[:, :kv_cols], w_in_bf[:, kv_cols:]
    pool_w_bf = pool_w[l].astype(BF16)
    wpa, wpb, wout = w_proj_a[l].astype(BF16), w_proj_b[l].astype(BF16), w_out[l].astype(BF16)
    wa, wb, wd = w_up_a[l].astype(BF16), w_up_b[l].astype(BF16), w_down[l].astype(BF16)

    hp = x_prompt.reshape(bp * t, d)
    u, q, kt, vt, kt_bf, v_bf = _in_proj(hp, norm_mix[l], w_qkv, tm=tm, seq_len=t, pw=pw, aw=aw,
                                         q_dtype=BF16)
    yb = _prompt_attention(q.reshape(bp, t, aw), kt_bf, v_bf.reshape(bp, t, aw), sb_bias[l],
                           tq=tq, tk=tk, groups=aw // LANES)
    hp, pool_tail = _mix_prompt(hp, u, yb.reshape(bp * t, aw), norm_mix[l], w_gates, pool_w_bf,
                                pool_scale[l], wpa, wpb, wout, tm=tm, seq_len=t)
    y_prompt, conv_tail = _ffn_prompt(hp, norm_ffn[l], wa, wb, conv_w[l], conv_b[l], wd, norm_final,
                                      tm=tm, seq_len=t)
    rows_major = lambda a: jnp.transpose(a.reshape(bp, n_heads, hd, t), (0, 3, 1, 2))
    k_prompt, v_prompt = rows_major(kt), rows_major(vt)
    pool_prompt = pool_tail[:, POOL_HALO - POOL_STATE:]
    conv_prompt = conv_tail[:, CONV_HALO - CONV_STATE:]

    hs = jnp.transpose(x_sample, (1, 0, 2)).reshape(ts * bs, d)
    us, qs, ksn, vsn = _in_proj(hs, norm_mix[l], w_qkv, tm=_tile(ts * bs, 256), seq_len=None,
                                pw=pw, aw=aw, q_dtype=F32)
    to_bm = lambda a: jnp.transpose(a.reshape(ts, bs, -1), (1, 0, 2))
    ks_bm, vs_bm = to_bm(ksn), to_bm(vsn)
    ybs = _decode_attention(to_bm(qs), ks_bm, vs_bm, pages_keys_last(cache_k[l]),
                            pages_keys_last(cache_v[l]), page_table, sb_bias[l], pages_per_step=pps)
    ybs_tm = jnp.transpose(ybs, (1, 0, 2)).reshape(ts * bs, aw)
    pool_tm = jnp.transpose(state_pool[l], (1, 0, 2)).reshape(POOL_STATE * bs, pw)
    hs = _mix_sample(hs, pool_tm, us, ybs_tm, norm_mix[l], w_gates, pool_w_bf, pool_scale[l],
                     wpa, wpb, wout, bs=bs, ts=ts, past_len=past_len)
    conv_tm = jnp.transpose(state_conv[l], (1, 0, 2)).reshape(CONV_STATE * bs, -1)
    hs_out, a_tail = _ffn_sample(hs, conv_tm, norm_ffn[l], wa, wb, conv_w[l], conv_b[l], wd,
                                 norm_final, bs=bs)
    y_sample = jnp.transpose(hs_out.reshape(ts, bs, d), (1, 0, 2))
    k_sample = ks_bm.reshape(bs, ts, n_heads, hd)
    v_sample = vs_bm.reshape(bs, ts, n_heads, hd)
    pool_sample = jnp.concatenate([state_pool[l], to_bm(us)], axis=1)[:, -POOL_STATE:]
    conv_sample = jnp.transpose(a_tail.reshape(CONV_STATE, bs, -1), (1, 0, 2))

    layer = lambda a: a[None]
    return (y_prompt.reshape(bp, t, d), y_sample, layer(k_prompt), layer(v_prompt),
            layer(pool_prompt), layer(conv_prompt), layer(k_sample), layer(v_sample),
            layer(pool_sample), layer(conv_sample))
```

```python
import functools
import math

import jax
import jax.numpy as jnp
from jax import lax
from jax.experimental import pallas as pl
from jax.experimental.pallas import tpu as pltpu

EPS = 1e-6
HEAD_DIM = 64
LANES = 128
HEADS_PER_GROUP = LANES // HEAD_DIM
POOL_WINDOWS = (2, 4, 8, 16)
POOL_STATE = max(POOL_WINDOWS) - 1
POOL_HALO = 16
CONV_WIDTH = 3
CONV_STATE = CONV_WIDTH - 1
CONV_HALO = 8
VMEM_LIMIT = 56 * 1024 * 1024
BF16 = jnp.bfloat16
F32 = jnp.float32
LOG2E = math.log2(math.e)
Q_SCALE = HEAD_DIM ** -0.5 * LOG2E


def _cparams(*sem):
    return pltpu.CompilerParams(dimension_semantics=sem, vmem_limit_bytes=VMEM_LIMIT)


def _const_spec(shape):
    nd = len(shape)
    return pl.BlockSpec(shape, lambda *_: (0,) * nd)


def _resident_spec(shape):
    nd = len(shape)
    return pl.BlockSpec(shape, lambda *_: (0,) * nd, pipeline_mode=pl.Buffered(1))


def _rmsnorm(x, g):
    return x * lax.rsqrt(jnp.mean(x * x, axis=-1, keepdims=True) + EPS) * g


def _dot(a, b):
    return jnp.dot(a, b, preferred_element_type=F32)


def _softplus2(z):
    neg_abs = pltpu.bitcast(pltpu.bitcast(z, jnp.int32) | jnp.int32(-2 ** 31), F32)
    return jnp.maximum(z, 0.0) + jnp.log2(1.0 + jnp.exp2(neg_abs))


def _in_proj_kernel(x_ref, g_ref, w_ref, u_ref, q_ref, k_ref, v_ref, *kv_refs, pw, aw):
    h = _rmsnorm(x_ref[...], g_ref[...]).astype(BF16)
    o1, o2, o3, o4 = pw, pw + aw, pw + 2 * aw, pw + 3 * aw
    u_ref[...] = _dot(h, w_ref[:, :o1])
    q_ref[...] = (_dot(h, w_ref[:, o1:o2]) * Q_SCALE).astype(q_ref.dtype)
    k = _dot(h, w_ref[:, o2:o3])
    v = _dot(h, w_ref[:, o3:o4])
    if kv_refs:
        ktb_ref, vb_ref = kv_refs
        kt = k.T
        k_ref[0] = kt
        v_ref[0] = v.T
        ktb_ref[0] = kt.astype(BF16)
        vb_ref[...] = v.astype(BF16)
    else:
        k_ref[...] = k
        v_ref[...] = v


def _in_proj(x2d, norm_g, w_in_bf, *, tm, seq_len, pw, aw, q_dtype):
    n, d = x2d.shape
    inw = w_in_bf.shape[1]
    grid = (n // tm,)
    row = lambda w: pl.BlockSpec((tm, w), lambda i: (i, 0))
    kv_shape, kv_spec = jax.ShapeDtypeStruct((n, aw), F32), row(aw)
    extra_shape, extra_specs = [], []
    if seq_len is not None:
        tps = seq_len // tm
        kv_shape = jax.ShapeDtypeStruct((n // seq_len, aw, seq_len), F32)
        kv_spec = pl.BlockSpec((1, aw, tm), lambda i: (i // tps, 0, i % tps))
        extra_shape = [jax.ShapeDtypeStruct((n // seq_len, aw, seq_len), BF16),
                       jax.ShapeDtypeStruct((n, aw), BF16)]
        extra_specs = [kv_spec, row(aw)]
    out_shape = [jax.ShapeDtypeStruct((n, pw), F32), jax.ShapeDtypeStruct((n, aw), q_dtype),
                 kv_shape, kv_shape] + extra_shape
    out_specs = [row(pw), row(aw), kv_spec, kv_spec] + extra_specs
    return pl.pallas_call(
        functools.partial(_in_proj_kernel, pw=pw, aw=aw),
        grid=grid,
        in_specs=[row(d), _resident_spec((1, d)), _resident_spec((d, inw))],
        out_specs=out_specs,
        out_shape=out_shape,
        compiler_params=_cparams("parallel"),
        name="in_proj",
    )(x2d, norm_g.reshape(1, d), w_in_bf)


def _sb_tile(z, tri, carry, mask, lane_sums=False):
    sp = _softplus2(z)
    if mask is not None:
        sp = jnp.where(mask, sp, 0.0)
    sp_bf = sp.astype(BF16)
    right = _dot(sp_bf, tri)
    if lane_sums:
        tot = _dot(sp_bf, jnp.ones((z.shape[1], LANES), BF16))
        carry = jnp.concatenate([carry] * (z.shape[1] // LANES), axis=1)
    else:
        tot = jnp.sum(sp, axis=1, keepdims=True)
    w = jnp.exp2(z - sp - right - carry)
    if mask is not None:
        w = jnp.where(mask, w, 0.0)
    return w, tot


def _split3_bf16(x):
    hi = x.astype(BF16).astype(F32)
    mid = (x - hi).astype(BF16).astype(F32)
    return hi, mid, x - hi - mid


def _prompt_attn_kernel(bias_ref, q_ref, kt_ref, v_ref, tri_ref, ones_ref, o_ref, acc_ref, carry_ref,
                        *, tq, tk, groups):
    gi = pl.program_id(1)
    i = pl.program_id(2)
    n_heads = groups * HEADS_PER_GROUP
    lane = lax.broadcasted_iota(jnp.int32, (tq, LANES), 1)
    qh = []
    for p in range(groups):
        q2 = q_ref[0, :, p * LANES:(p + 1) * LANES]
        zero = jnp.zeros_like(q2)
        qh += [jnp.where(lane < HEAD_DIM, q2, zero), jnp.where(lane >= HEAD_DIM, q2, zero)]
    for h in range(n_heads):
        hi, mid, lo = _split3_bf16(jnp.full((tq, LANES), bias_ref[gi * n_heads + h] * LOG2E, F32))
        cols = jnp.where(lane == 0, hi, jnp.where(lane == 1, mid, jnp.where(lane == 2, lo, 0.0)))
        qh[h] = jnp.concatenate([qh[h], cols.astype(BF16)], axis=1)
    tri = tri_ref[...]
    acc_ref[...] = jnp.zeros_like(acc_ref)
    carry_ref[...] = jnp.zeros_like(carry_ref)
    ratio = tq // tk

    def step(j, r0, masked):
        start = pl.multiple_of(j * tk, tk)
        mask = None
        if masked:
            row = lax.broadcasted_iota(jnp.int32, (tq - r0, tk), 0)
            col = lax.broadcasted_iota(jnp.int32, (tq - r0, tk), 1)
            mask = col < row
        for h in range(n_heads):
            p = h // HEADS_PER_GROUP
            kt = jnp.concatenate([kt_ref[0, p * LANES:(p + 1) * LANES, pl.ds(start, tk)],
                                  ones_ref[...]], axis=0)
            vv = v_ref[0, pl.ds(start, tk), p * LANES:(p + 1) * LANES]
            z = _dot(qh[h][r0:], kt)
            w, tot = _sb_tile(z, tri, carry_ref[h, r0:], mask, lane_sums=True)
            acc_ref[h, r0:] += _dot(w.astype(BF16), vv)
            carry_ref[h, r0:] += tot

    for c in range(ratio - 1, -1, -1):
        step(i * ratio + c, c * tk, True)

    def body(it, carry):
        step(i * ratio - 1 - it, 0, False)
        return carry

    lax.fori_loop(0, i * ratio, body, 0)
    for p in range(groups):
        o_ref[0, :, p * LANES:(p + 1) * LANES] = jnp.where(
            lane < HEAD_DIM, acc_ref[2 * p], acc_ref[2 * p + 1]).astype(o_ref.dtype)


def _strict_tri(n):
    r = lax.broadcasted_iota(jnp.int32, (n, n), 0)
    c = lax.broadcasted_iota(jnp.int32, (n, n), 1)
    return (r > c).astype(BF16)


def _prompt_attention(q_bf, kt_bf, v_bf, sb_bias, *, tq, tk, groups):
    b, t, aw = q_bf.shape
    w = groups * LANES
    bias_rows = (lax.broadcasted_iota(jnp.int32, (LANES, tk), 0) < 3).astype(BF16)
    return pl.pallas_call(
        functools.partial(_prompt_attn_kernel, tq=tq, tk=tk, groups=groups),
        grid=(b, aw // w, t // tq),
        in_specs=[pl.BlockSpec(memory_space=pltpu.SMEM),
                  pl.BlockSpec((1, tq, w), lambda bi, g, i: (bi, i, g)),
                  pl.BlockSpec((1, w, t), lambda bi, g, i: (bi, g, 0)),
                  pl.BlockSpec((1, t, w), lambda bi, g, i: (bi, 0, g)),
                  _const_spec((tk, tk)), _const_spec((LANES, tk))],
        out_specs=pl.BlockSpec((1, tq, w), lambda bi, g, i: (bi, i, g)),
        out_shape=jax.ShapeDtypeStruct((b, t, aw), BF16),
        scratch_shapes=[pltpu.VMEM((groups * HEADS_PER_GROUP, tq, LANES), F32),
                        pltpu.VMEM((groups * HEADS_PER_GROUP, tq, LANES), F32)],
        compiler_params=_cparams("parallel", "parallel", "parallel"),
        name="prompt_attn",
    )(sb_bias, q_bf, kt_bf, v_bf, _strict_tri(tk), bias_rows)


def _decode_attn_kernel(pt_ref, q_ref, kn_ref, vn_ref, bias_ref, tri_ref, *rest,
                        pages_per_step, page, n_heads, dec_seq, chunk):
    k_refs = rest[:pages_per_step]
    v_refs = rest[pages_per_step:2 * pages_per_step]
    o_ref = rest[2 * pages_per_step]
    qbd_ref, kbuf_ref, vbuf_ref, acc_ref, carry_ref = rest[2 * pages_per_step + 1:]
    j = pl.program_id(1)
    rows = n_heads * dec_seq
    aw = n_heads * HEAD_DIM
    bias = bias_ref[...][:, :1]
    tri = tri_ref[...]

    def tile(kk_t, vv_t, mask):
        n = kk_t.shape[1]
        z = _dot(qbd_ref[...], kk_t) + bias
        w, tot = _sb_tile(z, tri[:n, :n], carry_ref[...], mask)
        acc_ref[...] += lax.dot_general(w.astype(BF16), vv_t, (((1,), (1,)), ((), ())),
                                        preferred_element_type=F32)
        carry_ref[...] += tot

    @pl.when(j == 0)
    def _():
        q = q_ref[0]
        qt = jnp.concatenate([q] * n_heads, axis=0)
        r = lax.broadcasted_iota(jnp.int32, (rows, aw), 0)
        c = lax.broadcasted_iota(jnp.int32, (rows, aw), 1)
        qbd_ref[...] = jnp.where(c // HEAD_DIM == r // dec_seq, qt, 0.0).astype(BF16)
        acc_ref[...] = jnp.zeros_like(acc_ref)
        carry_ref[...] = jnp.zeros_like(carry_ref)
        pad = jnp.zeros((page - dec_seq, aw), F32)
        kn_t = jnp.concatenate([kn_ref[0], pad], axis=0).T.astype(BF16)
        vn_t = jnp.concatenate([vn_ref[0], pad], axis=0).T.astype(BF16)
        rq = lax.broadcasted_iota(jnp.int32, (rows, page), 0) % dec_seq
        ck = lax.broadcasted_iota(jnp.int32, (rows, page), 1)
        tile(kn_t, vn_t, ck < rq)

    for p in range(pages_per_step):
        dst = (pages_per_step - 1 - p) * page
        kbuf_ref[:, dst:dst + page] = k_refs[p][0].astype(BF16)
        vbuf_ref[:, dst:dst + page] = v_refs[p][0].astype(BF16)
    keys = pages_per_step * page
    z = _dot(qbd_ref[...], kbuf_ref[...]) + bias
    sp = _softplus2(z)
    starts = list(range(0, keys, chunk))
    sp_c = [sp[:, c0:c0 + chunk] for c0 in starts]
    right = [_dot(s.astype(BF16), tri) for s in sp_c]
    totals = [jnp.sum(s, axis=1, keepdims=True) for s in sp_c]
    carry = carry_ref[...]
    ws = [None] * len(starts)
    for ci in range(len(starts) - 1, -1, -1):
        c0 = starts[ci]
        ws[ci] = jnp.exp2(z[:, c0:c0 + chunk] - sp_c[ci] - right[ci] - carry).astype(BF16)
        carry = carry + totals[ci]
    acc_ref[...] += lax.dot_general(jnp.concatenate(ws, axis=1), vbuf_ref[...],
                                    (((1,), (1,)), ((), ())), preferred_element_type=F32)
    carry_ref[...] = carry

    @pl.when(j == pl.num_programs(1) - 1)
    def _():
        acc = acc_ref[...]
        r = lax.broadcasted_iota(jnp.int32, (rows, aw), 0)
        c = lax.broadcasted_iota(jnp.int32, (rows, aw), 1)
        sel = jnp.where(c // HEAD_DIM == r // dec_seq, acc, 0.0)
        out = sel[:dec_seq]
        for h in range(1, n_heads):
            out = out + sel[h * dec_seq:(h + 1) * dec_seq]
        o_ref[0] = out.astype(o_ref.dtype)


def _decode_attention(q_s, k_new, v_new, cache_k, cache_v, page_table, sb_bias, *, pages_per_step):
    bs, ts, aw = q_s.shape
    n_phys, _, page = cache_k.shape
    n_pages = page_table.shape[1]
    n_heads = aw // HEAD_DIM
    rows = n_heads * ts
    steps = n_pages // pages_per_step
    chunk = min(2 * page, pages_per_step * page)
    bias_rows = jnp.broadcast_to(jnp.repeat(sb_bias * LOG2E, ts)[:, None], (rows, LANES)).astype(F32)

    def page_spec(p):
        def imap(b, j, pt):
            return (pt[b * n_pages + (n_pages - 1 - (j * pages_per_step + p))], 0, 0)
        return pl.BlockSpec((1, aw, page), imap)

    seq_spec = pl.BlockSpec((1, ts, aw), lambda b, j, pt: (b, 0, 0))
    grid_spec = pltpu.PrefetchScalarGridSpec(
        num_scalar_prefetch=1,
        grid=(bs, steps),
        in_specs=[seq_spec, seq_spec, seq_spec,
                  pl.BlockSpec((rows, LANES), lambda b, j, pt: (0, 0)),
                  pl.BlockSpec((chunk, chunk), lambda b, j, pt: (0, 0))]
                 + [page_spec(p) for p in range(pages_per_step)] * 2,
        out_specs=seq_spec,
        scratch_shapes=[pltpu.VMEM((rows, aw), BF16),
                        pltpu.VMEM((aw, pages_per_step * page), BF16),
                        pltpu.VMEM((aw, pages_per_step * page), BF16),
                        pltpu.VMEM((rows, aw), F32),
                        pltpu.VMEM((rows, 1), F32)],
    )
    return pl.pallas_call(
        functools.partial(_decode_attn_kernel, pages_per_step=pages_per_step, page=page,
                          n_heads=n_heads, dec_seq=ts, chunk=chunk),
        grid_spec=grid_spec,
        out_shape=jax.ShapeDtypeStruct((bs, ts, aw), F32),
        compiler_params=_cparams("parallel", "arbitrary"),
        name="decode_attn",
    )(page_table.reshape(-1), q_s, k_new, v_new, bias_rows, _strict_tri(chunk),
      *([cache_k] * pages_per_step), *([cache_v] * pages_per_step))


def _mix_tail(x, y_pool_groups, yb, nm_ref, wg_ref, pw_ref, ps_ref, wpa_ref, wpb_ref, wout_ref):
    d = x.shape[1]
    h = _rmsnorm(x, nm_ref[...]).astype(BF16)
    ga = jax.nn.sigmoid(_dot(h, wg_ref[:, :d]))
    gb = jax.nn.sigmoid(_dot(h, wg_ref[:, d:]))
    mixed = [_dot(p.astype(BF16), pw_ref[g]) for g, p in enumerate(y_pool_groups)]
    y_a = jnp.concatenate(mixed, axis=1) * ps_ref[...]
    m = ga * _dot(y_a.astype(BF16), wpa_ref[...]) + gb * _dot(yb.astype(BF16), wpb_ref[...])
    return x + _dot(m.astype(BF16), wout_ref[...])


def _mix_prompt_kernel(x_ref, u_ref, yb_ref, nm_ref, wg_ref, pw_ref, ps_ref, wpa_ref, wpb_ref,
                       wout_ref, o_ref, st_ref, halo_ref, *, tps, tm):
    i = pl.program_id(0)
    first = (i % tps) == 0

    @pl.when(first)
    def _():
        halo_ref[...] = jnp.zeros_like(halo_ref)

    u = u_ref[...]
    ext = jnp.concatenate([halo_ref[...], u], axis=0)
    pos = (i % tps) * tm + lax.broadcasted_iota(jnp.int32, (tm, LANES), 0)
    groups = []
    for g, w in enumerate(POOL_WINDOWS):
        e = ext[:, g * LANES:(g + 1) * LANES]
        s, span = e, 1
        while span < w:
            s = s + pltpu.roll(s, span, 0)
            span *= 2
        cnt = jnp.minimum(pos + 1, w).astype(F32)
        groups.append(s[POOL_HALO:] / cnt - e[POOL_HALO:])
    o_ref[...] = _mix_tail(x_ref[...], groups, yb_ref[...], nm_ref, wg_ref,
                           pw_ref, ps_ref, wpa_ref, wpb_ref, wout_ref)
    tail = u[tm - POOL_HALO:]
    halo_ref[...] = tail
    st_ref[0] = tail


def _mix_prompt(x2d, u, yb, norm_g, w_gates, pool_w_bf, pool_scale, wpa, wpb, wout, *, tm, seq_len):
    n, d = x2d.shape
    pw = u.shape[1]
    aw = yb.shape[1]
    tps = seq_len // tm
    row = lambda w: pl.BlockSpec((tm, w), lambda i: (i, 0))
    return pl.pallas_call(
        functools.partial(_mix_prompt_kernel, tps=tps, tm=tm),
        grid=(n // tm,),
        in_specs=[row(d), row(pw), row(aw), _resident_spec((1, d)), _resident_spec(w_gates.shape),
                  _resident_spec(pool_w_bf.shape), _resident_spec((1, pw)),
                  _resident_spec(wpa.shape), _resident_spec(wpb.shape), _resident_spec(wout.shape)],
        out_specs=[row(d), pl.BlockSpec((1, POOL_HALO, pw), lambda i: (i // tps, 0, 0))],
        out_shape=[jax.ShapeDtypeStruct((n, d), F32),
                   jax.ShapeDtypeStruct((n // seq_len, POOL_HALO, pw), F32)],
        scratch_shapes=[pltpu.VMEM((POOL_HALO, pw), F32)],
        compiler_params=_cparams("arbitrary"),
        name="mix_prompt",
    )(x2d, u, yb, norm_g.reshape(1, d), w_gates, pool_w_bf, pool_scale.reshape(1, pw), wpa, wpb, wout)


def _mix_sample_kernel(x_ref, st_ref, u_ref, yb_ref, nm_ref, wg_ref, pw_ref, ps_ref, wpa_ref, wpb_ref,
                       wout_ref, o_ref, *, bs, ts, past_len):
    slabs = [st_ref[j * bs:(j + 1) * bs, :] for j in range(POOL_STATE)]
    slabs += [u_ref[t * bs:(t + 1) * bs, :] for t in range(ts)]
    groups = []
    for g, w in enumerate(POOL_WINDOWS):
        e = [s[:, g * LANES:(g + 1) * LANES] for s in slabs]
        s, span = e, 1
        while span < w:
            s = [s[j] + s[j - span] if j >= span else s[j] for j in range(len(s))]
            span *= 2
        rows = []
        for t in range(ts):
            cnt = float(min(past_len + t + 1, w))
            rows.append(s[POOL_STATE + t] / cnt - e[POOL_STATE + t])
        groups.append(jnp.concatenate(rows, axis=0))
    o_ref[...] = _mix_tail(x_ref[...], groups, yb_ref[...], nm_ref, wg_ref,
                           pw_ref, ps_ref, wpa_ref, wpb_ref, wout_ref)


def _mix_sample(x_tm, state_tm, u, yb, norm_g, w_gates, pool_w_bf, pool_scale, wpa, wpb, wout, *,
                bs, ts, past_len):
    n, d = x_tm.shape
    pw = u.shape[1]
    args = (x_tm, state_tm, u, yb, norm_g.reshape(1, d), w_gates, pool_w_bf, pool_scale.reshape(1, pw),
            wpa, wpb, wout)
    return pl.pallas_call(
        functools.partial(_mix_sample_kernel, bs=bs, ts=ts, past_len=past_len),
        grid=(1,),
        in_specs=[_const_spec(a.shape) for a in args],
        out_specs=_const_spec((n, d)),
        out_shape=jax.ShapeDtypeStruct((n, d), F32),
        compiler_params=_cparams("arbitrary"),
        name="mix_sample",
    )(*args)


def _gelu_tanh(x):
    return 0.5 * x * (1.0 + jnp.tanh(0.7978845608028654 * (x + 0.044715 * (x * x * x))))


def _ffn_tail(x, a0, a1, a2, gate, cw_ref, cb_ref, wd_ref, nf_ref):
    conv = cb_ref[...] + a0 * cw_ref[0:1, :] + a1 * cw_ref[1:2, :] + a2 * cw_ref[2:3, :]
    hid = (_gelu_tanh(conv) * gate).astype(BF16)
    x2 = x + _dot(hid, wd_ref[...])
    return _rmsnorm(x2, nf_ref[...])


def _ffn_prompt_kernel(x_ref, g_ref, wa_ref, wb_ref, cw_ref, cb_ref, wd_ref, nf_ref, o_ref, st_ref,
                       halo_ref, *, tps, tm):
    i = pl.program_id(0)

    @pl.when((i % tps) == 0)
    def _():
        halo_ref[...] = jnp.zeros_like(halo_ref)

    x = x_ref[...]
    h = _rmsnorm(x, g_ref[...]).astype(BF16)
    a = _dot(h, wa_ref[...])
    gate = _dot(h, wb_ref[...])
    ext = jnp.concatenate([halo_ref[...], a], axis=0)
    a1 = pltpu.roll(ext, 1, 0)[CONV_HALO:]
    a0 = pltpu.roll(ext, 2, 0)[CONV_HALO:]
    o_ref[...] = _ffn_tail(x, a0, a1, a, gate, cw_ref, cb_ref, wd_ref, nf_ref)
    tail = a[tm - CONV_HALO:]
    halo_ref[...] = tail
    st_ref[0] = tail


def _ffn_prompt(x2d, norm_g, wa, wb, conv_w, conv_b, wd, norm_final, *, tm, seq_len):
    n, d = x2d.shape
    f = wa.shape[1]
    tps = seq_len // tm
    row = pl.BlockSpec((tm, d), lambda i: (i, 0))
    return pl.pallas_call(
        functools.partial(_ffn_prompt_kernel, tps=tps, tm=tm),
        grid=(n // tm,),
        in_specs=[row, _resident_spec((1, d)), _resident_spec(wa.shape), _resident_spec(wb.shape),
                  _resident_spec(conv_w.shape), _resident_spec((1, f)), _resident_spec(wd.shape),
                  _resident_spec((1, d))],
        out_specs=[row, pl.BlockSpec((1, CONV_HALO, f), lambda i: (i // tps, 0, 0))],
        out_shape=[jax.ShapeDtypeStruct((n, d), F32),
                   jax.ShapeDtypeStruct((n // seq_len, CONV_HALO, f), F32)],
        scratch_shapes=[pltpu.VMEM((CONV_HALO, f), F32)],
        compiler_params=_cparams("arbitrary"),
        name="ffn_prompt",
    )(x2d, norm_g.reshape(1, d), wa, wb, conv_w, conv_b.reshape(1, f), wd, norm_final.reshape(1, d))


def _ffn_sample_kernel(x_ref, st_ref, g_ref, wa_ref, wb_ref, cw_ref, cb_ref, wd_ref, nf_ref,
                       o_ref, a_ref, *, bs):
    x = x_ref[...]
    n = x.shape[0]
    h = _rmsnorm(x, g_ref[...]).astype(BF16)
    a = _dot(h, wa_ref[...])
    gate = _dot(h, wb_ref[...])
    ext = jnp.concatenate([st_ref[...], a], axis=0)
    a0 = ext[:n]
    a1 = ext[bs:bs + n]
    o_ref[...] = _ffn_tail(x, a0, a1, a, gate, cw_ref, cb_ref, wd_ref, nf_ref)
    a_ref[...] = ext[n:]


def _ffn_sample(x_tm, state_tm, norm_g, wa, wb, conv_w, conv_b, wd, norm_final, *, bs):
    n, d = x_tm.shape
    f = wa.shape[1]
    args = (x_tm, state_tm, norm_g.reshape(1, d), wa, wb, conv_w, conv_b.reshape(1, f), wd,
            norm_final.reshape(1, d))
    return pl.pallas_call(
        functools.partial(_ffn_sample_kernel, bs=bs),
        grid=(1,),
        in_specs=[_const_spec(a.shape) for a in args],
        out_specs=[_const_spec((n, d)), _const_spec((CONV_STATE * bs, f))],
        out_shape=[jax.ShapeDtypeStruct((n, d), F32), jax.ShapeDtypeStruct((CONV_STATE * bs, f), F32)],
        compiler_params=_cparams("arbitrary"),
        name="ffn_sample",
    )(*args)


def _tile(n, pref):
    t = min(pref, n)
    assert n % t == 0, (n, t)
    return t


def kernel(x_prompt, x_sample, cache_k, cache_v, page_table, state_pool, state_conv, norm_mix, w_in, sb_bias, pool_w, pool_scale, w_proj_a, w_proj_b, w_out, norm_ffn, w_up_a, w_up_b, conv_w, conv_b, w_down, norm_final):
    depth = norm_mix.shape[0]
    bp, t, d = x_prompt.shape
    bs, ts, _ = x_sample.shape
    n_heads, hd = cache_k.shape[-2:]
    assert hd == HEAD_DIM
    aw = n_heads * hd
    pw = pool_scale.shape[-1]
    n_phys, page = cache_k.shape[1:3]
    past_len = page_table.shape[1] * page
    assert pw == LANES * len(POOL_WINDOWS) and state_pool.shape[2] == POOL_STATE
    assert state_conv.shape[2] == CONV_STATE and conv_w.shape[1] == CONV_WIDTH

    tm = _tile(t, 512)
    tk = _tile(t, 256)
    tq = _tile(t, 512)
    pps = _tile(page_table.shape[1], 32)

    def pages_keys_last(cache):
        return jnp.transpose(cache, (0, 2, 3, 1)).reshape(n_phys, aw, page)

    assert depth == 1, "the final norm is fused into the (single) layer's FFN kernel"
    l = 0
    kv_cols = pw + 3 * aw
    w_in_bf = w_in[l].astype(BF16)
    w_qkv, w_gates = w_in_bf[:, :kv_cols], w_in_bf[:, kv_cols:]
    pool_w_bf = pool_w[l].astype(BF16)
    wpa, wpb, wout = w_proj_a[l].astype(BF16), w_proj_b[l].astype(BF16), w_out[l].astype(BF16)
    wa, wb, wd = w_up_a[l].astype(BF16), w_up_b[l].astype(BF16), w_down[l].astype(BF16)

    hp = x_prompt.reshape(bp * t, d)
    u, q, kt, vt, kt_bf, v_bf = _in_proj(hp, norm_mix[l], w_qkv, tm=tm, seq_len=t, pw=pw, aw=aw,
                                         q_dtype=BF16)
    yb = _prompt_attention(q.reshape(bp, t, aw), kt_bf, v_bf.reshape(bp, t, aw), sb_bias[l],
                           tq=tq, tk=tk, groups=aw // LANES)
    hp, pool_tail = _mix_prompt(hp, u, yb.reshape(bp * t, aw), norm_mix[l], w_gates, pool_w_bf,
                                pool_scale[l], wpa, wpb, wout, tm=tm, seq_len=t)
    y_prompt, conv_tail = _ffn_prompt(hp, norm_ffn[l], wa, wb, conv_w[l], conv_b[l], wd, norm_final,
                                      tm=tm, seq_len=t)
    rows_major = lambda a: jnp.transpose(a.reshape(bp, n_heads, hd, t), (0, 3, 1, 2))
    k_prompt, v_prompt = rows_major(kt), rows_major(vt)
    pool_prompt = pool_tail[:, POOL_HALO - POOL_STATE:]
    conv_prompt = conv_tail[:, CONV_HALO - CONV_STATE:]

    hs = jnp.transpose(x_sample, (1, 0, 2)).reshape(ts * bs, d)
    us, qs, ksn, vsn = _in_proj(hs, norm_mix[l], w_qkv, tm=_tile(ts * bs, 256), seq_len=None,
                                pw=pw, aw=aw, q_dtype=F32)
    to_bm = lambda a: jnp.transpose(a.reshape(ts, bs, -1), (1, 0, 2))
    ks_bm, vs_bm = to_bm(ksn), to_bm(vsn)
    ybs = _decode_attention(to_bm(qs), ks_bm, vs_bm, pages_keys_last(cache_k[l]),
                            pages_keys_last(cache_v[l]), page_table, sb_bias[l], pages_per_step=pps)
    ybs_tm = jnp.transpose(ybs, (1, 0, 2)).reshape(ts * bs, aw)
    pool_tm = jnp.transpose(state_pool[l], (1, 0, 2)).reshape(POOL_STATE * bs, pw)
    hs = _mix_sample(hs, pool_tm, us, ybs_tm, norm_mix[l], w_gates, pool_w_bf, pool_scale[l],
                     wpa, wpb, wout, bs=bs, ts=ts, past_len=past_len)
    conv_tm = jnp.transpose(state_conv[l], (1, 0, 2)).reshape(CONV_STATE * bs, -1)
    hs_out, a_tail = _ffn_sample(hs, conv_tm, norm_ffn[l], wa, wb, conv_w[l], conv_b[l], wd,
                                 norm_final, bs=bs)
    y_sample = jnp.transpose(hs_out.reshape(ts, bs, d), (1, 0, 2))
    k_sample = ks_bm.reshape(bs, ts, n_heads, hd)
    v_sample = vs_bm.reshape(bs, ts, n_heads, hd)
    pool_sample = jnp.concatenate([state_pool[l], to_bm(us)], axis=1)[:, -POOL_STATE:]
    conv_sample = jnp.transpose(a_tail.reshape(CONV_STATE, bs, -1), (1, 0, 2))

    layer = lambda a: a[None]
    return (y_prompt.reshape(bp, t, d), y_sample, layer(k_prompt), layer(v_prompt),
            layer(pool_prompt), layer(conv_prompt), layer(k_sample), layer(v_sample),
            layer(pool_sample), layer(conv_sample))
```

```python
import functools
import math

import jax
import jax.numpy as jnp
from jax import lax
from jax.experimental import pallas as pl
from jax.experimental.pallas import tpu as pltpu

EPS = 1e-6
HEAD_DIM = 64
LANES = 128
HEADS_PER_GROUP = LANES // HEAD_DIM
POOL_WINDOWS = (2, 4, 8, 16)
POOL_STATE = max(POOL_WINDOWS) - 1
POOL_HALO = 16
CONV_WIDTH = 3
CONV_STATE = CONV_WIDTH - 1
CONV_HALO = 8
VMEM_LIMIT = 56 * 1024 * 1024
BF16 = jnp.bfloat16
F32 = jnp.float32
LOG2E = math.log2(math.e)
Q_SCALE = HEAD_DIM ** -0.5 * LOG2E


def _cparams(*sem):
    return pltpu.CompilerParams(dimension_semantics=sem, vmem_limit_bytes=VMEM_LIMIT)


def _const_spec(shape):
    nd = len(shape)
    return pl.BlockSpec(shape, lambda *_: (0,) * nd)


def _resident_spec(shape):
    nd = len(shape)
    return pl.BlockSpec(shape, lambda *_: (0,) * nd, pipeline_mode=pl.Buffered(1))


def _rmsnorm(x, g):
    return x * lax.rsqrt(jnp.mean(x * x, axis=-1, keepdims=True) + EPS) * g


def _dot(a, b):
    return jnp.dot(a, b, preferred_element_type=F32)


def _softplus2(z):
    neg_abs = pltpu.bitcast(pltpu.bitcast(z, jnp.int32) | jnp.int32(-2 ** 31), F32)
    return jnp.maximum(z, 0.0) + jnp.log2(1.0 + jnp.exp2(neg_abs))


def _in_proj_kernel(x_ref, g_ref, w_ref, u_ref, q_ref, k_ref, v_ref, *kv_refs, pw, aw):
    h = _rmsnorm(x_ref[...], g_ref[...]).astype(BF16)
    o1, o2, o3, o4 = pw, pw + aw, pw + 2 * aw, pw + 3 * aw
    u_ref[...] = _dot(h, w_ref[:, :o1])
    q_ref[...] = (_dot(h, w_ref[:, o1:o2]) * Q_SCALE).astype(q_ref.dtype)
    k = _dot(h, w_ref[:, o2:o3])
    v = _dot(h, w_ref[:, o3:o4])
    if kv_refs:
        ktb_ref, vb_ref = kv_refs
        kt = k.T
        k_ref[0] = kt
        v_ref[0] = v.T
        ktb_ref[0] = kt.astype(BF16)
        vb_ref[...] = v.astype(BF16)
    else:
        k_ref[...] = k
        v_ref[...] = v


def _in_proj(x2d, norm_g, w_in_bf, *, tm, seq_len, pw, aw, q_dtype):
    n, d = x2d.shape
    inw = w_in_bf.shape[1]
    grid = (n // tm,)
    row = lambda w: pl.BlockSpec((tm, w), lambda i: (i, 0))
    kv_shape, kv_spec = jax.ShapeDtypeStruct((n, aw), F32), row(aw)
    extra_shape, extra_specs = [], []
    if seq_len is not None:
        tps = seq_len // tm
        kv_shape = jax.ShapeDtypeStruct((n // seq_len, aw, seq_len), F32)
        kv_spec = pl.BlockSpec((1, aw, tm), lambda i: (i // tps, 0, i % tps))
        extra_shape = [jax.ShapeDtypeStruct((n // seq_len, aw, seq_len), BF16),
                       jax.ShapeDtypeStruct((n, aw), BF16)]
        extra_specs = [kv_spec, row(aw)]
    out_shape = [jax.ShapeDtypeStruct((n, pw), F32), jax.ShapeDtypeStruct((n, aw), q_dtype),
                 kv_shape, kv_shape] + extra_shape
    out_specs = [row(pw), row(aw), kv_spec, kv_spec] + extra_specs
    return pl.pallas_call(
        functools.partial(_in_proj_kernel, pw=pw, aw=aw),
        grid=grid,
        in_specs=[row(d), _resident_spec((1, d)), _resident_spec((d, inw))],
        out_specs=out_specs,
        out_shape=out_shape,
        compiler_params=_cparams("parallel"),
        name="in_proj",
    )(x2d, norm_g.reshape(1, d), w_in_bf)


def _sb_tile(z, tri, carry, mask, lane_sums=False):
    sp = _softplus2(z)
    if mask is not None:
        sp = jnp.where(mask, sp, 0.0)
    sp_bf = sp.astype(BF16)
    right = _dot(sp_bf, tri)
    if lane_sums:
        tot = _dot(sp_bf, jnp.ones((z.shape[1], LANES), BF16))
        carry = jnp.concatenate([carry] * (z.shape[1] // LANES), axis=1)
    else:
        tot = jnp.sum(sp, axis=1, keepdims=True)
    w = jnp.exp2(z - sp - right - carry)
    if mask is not None:
        w = jnp.where(mask, w, 0.0)
    return w, tot


def _split3_bf16(x):
    hi = x.astype(BF16).astype(F32)
    mid = (x - hi).astype(BF16).astype(F32)
    return hi, mid, x - hi - mid


def _prompt_attn_kernel(bias_ref, q_ref, kt_ref, v_ref, tri_ref, ones_ref, o_ref, acc_ref, carry_ref,
                        *, tq, tk, groups):
    gi = pl.program_id(1)
    i = pl.program_id(2)
    n_heads = groups * HEADS_PER_GROUP
    lane = lax.broadcasted_iota(jnp.int32, (tq, LANES), 1)
    qh = []
    for p in range(groups):
        q2 = q_ref[0, :, p * LANES:(p + 1) * LANES]
        zero = jnp.zeros_like(q2)
        qh += [jnp.where(lane < HEAD_DIM, q2, zero), jnp.where(lane >= HEAD_DIM, q2, zero)]
    for h in range(n_heads):
        hi, mid, lo = _split3_bf16(jnp.full((tq, LANES), bias_ref[gi * n_heads + h] * LOG2E, F32))
        cols = jnp.where(lane == 0, hi, jnp.where(lane == 1, mid, jnp.where(lane == 2, lo, 0.0)))
        qh[h] = jnp.concatenate([qh[h], cols.astype(BF16)], axis=1)
    tri = tri_ref[...]
    acc_ref[...] = jnp.zeros_like(acc_ref)
    carry_ref[...] = jnp.zeros_like(carry_ref)
    ratio = tq // tk

    def step(j, r0, masked):
        start = pl.multiple_of(j * tk, tk)
        mask = None
        if masked:
            row = lax.broadcasted_iota(jnp.int32, (tq - r0, tk), 0)
            col = lax.broadcasted_iota(jnp.int32, (tq - r0, tk), 1)
            mask = col < row
        for h in range(n_heads):
            p = h // HEADS_PER_GROUP
            kt = jnp.concatenate([kt_ref[0, p * LANES:(p + 1) * LANES, pl.ds(start, tk)],
                                  ones_ref[...]], axis=0)
            vv = v_ref[0, pl.ds(start, tk), p * LANES:(p + 1) * LANES]
            z = _dot(qh[h][r0:], kt)
            w, tot = _sb_tile(z, tri, carry_ref[h, r0:], mask, lane_sums=True)
            acc_ref[h, r0:] += _dot(w.astype(BF16), vv)
            carry_ref[h, r0:] += tot

    for c in range(ratio - 1, -1, -1):
        step(i * ratio + c, c * tk, True)

    def body(it, carry):
        step(i * ratio - 1 - it, 0, False)
        return carry

    lax.fori_loop(0, i * ratio, body, 0)
    for p in range(groups):
        o_ref[0, :, p * LANES:(p + 1) * LANES] = jnp.where(
            lane < HEAD_DIM, acc_ref[2 * p], acc_ref[2 * p + 1]).astype(o_ref.dtype)


def _strict_tri(n):
    r = lax.broadcasted_iota(jnp.int32, (n, n), 0)
    c = lax.broadcasted_iota(jnp.int32, (n, n), 1)
    return (r > c).astype(BF16)


def _prompt_attention(q_bf, kt_bf, v_bf, sb_bias, *, tq, tk, groups):
    b, t, aw = q_bf.shape
    w = groups * LANES
    bias_rows = (lax.broadcasted_iota(jnp.int32, (LANES, tk), 0) < 3).astype(BF16)
    return pl.pallas_call(
        functools.partial(_prompt_attn_kernel, tq=tq, tk=tk, groups=groups),
        grid=(b, aw // w, t // tq),
        in_specs=[pl.BlockSpec(memory_space=pltpu.SMEM),
                  pl.BlockSpec((1, tq, w), lambda bi, g, i: (bi, i, g)),
                  pl.BlockSpec((1, w, t), lambda bi, g, i: (bi, g, 0)),
                  pl.BlockSpec((1, t, w), lambda bi, g, i: (bi, 0, g)),
                  _const_spec((tk, tk)), _const_spec((LANES, tk))],
        out_specs=pl.BlockSpec((1, tq, w), lambda bi, g, i: (bi, i, g)),
        out_shape=jax.ShapeDtypeStruct((b, t, aw), BF16),
        scratch_shapes=[pltpu.VMEM((groups * HEADS_PER_GROUP, tq, LANES), F32),
                        pltpu.VMEM((groups * HEADS_PER_GROUP, tq, LANES), F32)],
        compiler_params=_cparams("parallel", "parallel", "parallel"),
        name="prompt_attn",
    )(sb_bias, q_bf, kt_bf, v_bf, _strict_tri(tk), bias_rows)


def _decode_attn_kernel(pt_ref, q_ref, kn_ref, vn_ref, bias_ref, tri_ref, *rest,
                        pages_per_step, page, n_heads, dec_seq, chunk):
    k_refs = rest[:pages_per_step]
    v_refs = rest[pages_per_step:2 * pages_per_step]
    o_ref = rest[2 * pages_per_step]
    qbd_ref, kbuf_ref, vbuf_ref, acc_ref, carry_ref = rest[2 * pages_per_step + 1:]
    j = pl.program_id(1)
    rows = n_heads * dec_seq
    aw = n_heads * HEAD_DIM
    bias = bias_ref[...][:, :1]
    tri = tri_ref[...]

    def tile(kk_t, vv_t, mask):
        n = kk_t.shape[1]
        z = _dot(qbd_ref[...], kk_t) + bias
        w, tot = _sb_tile(z, tri[:n, :n], carry_ref[...], mask)
        acc_ref[...] += lax.dot_general(w.astype(BF16), vv_t, (((1,), (1,)), ((), ())),
                                        preferred_element_type=F32)
        carry_ref[...] += tot

    @pl.when(j == 0)
    def _():
        q = q_ref[0]
        qt = jnp.concatenate([q] * n_heads, axis=0)
        r = lax.broadcasted_iota(jnp.int32, (rows, aw), 0)
        c = lax.broadcasted_iota(jnp.int32, (rows, aw), 1)
        qbd_ref[...] = jnp.where(c // HEAD_DIM == r // dec_seq, qt, 0.0).astype(BF16)
        acc_ref[...] = jnp.zeros_like(acc_ref)
        carry_ref[...] = jnp.zeros_like(carry_ref)
        pad = jnp.zeros((page - dec_seq, aw), F32)
        kn_t = jnp.concatenate([kn_ref[0], pad], axis=0).T.astype(BF16)
        vn_t = jnp.concatenate([vn_ref[0], pad], axis=0).T.astype(BF16)
        rq = lax.broadcasted_iota(jnp.int32, (rows, page), 0) % dec_seq
        ck = lax.broadcasted_iota(jnp.int32, (rows, page), 1)
        tile(kn_t, vn_t, ck < rq)

    for p in range(pages_per_step):
        dst = (pages_per_step - 1 - p) * page
        kbuf_ref[:, dst:dst + page] = k_refs[p][0].astype(BF16)
        vbuf_ref[:, dst:dst + page] = v_refs[p][0].astype(BF16)
    keys = pages_per_step * page
    z = _dot(qbd_ref[...], kbuf_ref[...]) + bias
    sp = _softplus2(z)
    starts = list(range(0, keys, chunk))
    sp_c = [sp[:, c0:c0 + chunk] for c0 in starts]
    right = [_dot(s.astype(BF16), tri) for s in sp_c]
    totals = [jnp.sum(s, axis=1, keepdims=True) for s in sp_c]
    carry = carry_ref[...]
    ws = [None] * len(starts)
    for ci in range(len(starts) - 1, -1, -1):
        c0 = starts[ci]
        ws[ci] = jnp.exp2(z[:, c0:c0 + chunk] - sp_c[ci] - right[ci] - carry).astype(BF16)
        carry = carry + totals[ci]
    acc_ref[...] += lax.dot_general(jnp.concatenate(ws, axis=1), vbuf_ref[...],
                                    (((1,), (1,)), ((), ())), preferred_element_type=F32)
    carry_ref[...] = carry

    @pl.when(j == pl.num_programs(1) - 1)
    def _():
        acc = acc_ref[...]
        r = lax.broadcasted_iota(jnp.int32, (rows, aw), 0)
        c = lax.broadcasted_iota(jnp.int32, (rows, aw), 1)
        sel = jnp.where(c // HEAD_DIM == r // dec_seq, acc, 0.0)
        out = sel[:dec_seq]
        for h in range(1, n_heads):
            out = out + sel[h * dec_seq:(h + 1) * dec_seq]
        o_ref[0] = out.astype(o_ref.dtype)


def _decode_attention(q_s, k_new, v_new, cache_k, cache_v, page_table, sb_bias, *, pages_per_step):
    bs, ts, aw = q_s.shape
    n_phys, _, page = cache_k.shape
    n_pages = page_table.shape[1]
    n_heads = aw // HEAD_DIM
    rows = n_heads * ts
    steps = n_pages // pages_per_step
    chunk = min(2 * page, pages_per_step * page)
    bias_rows = jnp.broadcast_to(jnp.repeat(sb_bias * LOG2E, ts)[:, None], (rows, LANES)).astype(F32)

    def page_spec(p):
        def imap(b, j, pt):
            return (pt[b * n_pages + (n_pages - 1 - (j * pages_per_step + p))], 0, 0)
        return pl.BlockSpec((1, aw, page), imap)

    seq_spec = pl.BlockSpec((1, ts, aw), lambda b, j, pt: (b, 0, 0))
    grid_spec = pltpu.PrefetchScalarGridSpec(
        num_scalar_prefetch=1,
        grid=(bs, steps),
        in_specs=[seq_spec, seq_spec, seq_spec,
                  pl.BlockSpec((rows, LANES), lambda b, j, pt: (0, 0)),
                  pl.BlockSpec((chunk, chunk), lambda b, j, pt: (0, 0))]
                 + [page_spec(p) for p in range(pages_per_step)] * 2,
        out_specs=seq_spec,
        scratch_shapes=[pltpu.VMEM((rows, aw), BF16),
                        pltpu.VMEM((aw, pages_per_step * page), BF16),
                        pltpu.VMEM((aw, pages_per_step * page), BF16),
                        pltpu.VMEM((rows, aw), F32),
                        pltpu.VMEM((rows, 1), F32)],
    )
    return pl.pallas_call(
        functools.partial(_decode_attn_kernel, pages_per_step=pages_per_step, page=page,
                          n_heads=n_heads, dec_seq=ts, chunk=chunk),
        grid_spec=grid_spec,
        out_shape=jax.ShapeDtypeStruct((bs, ts, aw), F32),
        compiler_params=_cparams("parallel", "arbitrary"),
        name="decode_attn",
    )(page_table.reshape(-1), q_s, k_new, v_new, bias_rows, _strict_tri(chunk),
      *([cache_k] * pages_per_step), *([cache_v] * pages_per_step))


def _mix_tail(x, y_pool_groups, yb, nm_ref, wg_ref, pw_ref, ps_ref, wpa_ref, wpb_ref, wout_ref):
    d = x.shape[1]
    h = _rmsnorm(x, nm_ref[...]).astype(BF16)
    ga = jax.nn.sigmoid(_dot(h, wg_ref[:, :d]))
    gb = jax.nn.sigmoid(_dot(h, wg_ref[:, d:]))
    mixed = [_dot(p.astype(BF16), pw_ref[g]) for g, p in enumerate(y_pool_groups)]
    y_a = jnp.concatenate(mixed, axis=1) * ps_ref[...]
    m = ga * _dot(y_a.astype(BF16), wpa_ref[...]) + gb * _dot(yb.astype(BF16), wpb_ref[...])
    return x + _dot(m.astype(BF16), wout_ref[...])


def _mix_prompt_rows(x, u_ref, yb_ref, nm_ref, wg_ref, pw_ref, ps_ref, wpa_ref, wpb_ref, wout_ref,
                     st_ref, halo_ref, *, tps, tm):
    i = pl.program_id(0)
    first = (i % tps) == 0

    @pl.when(first)
    def _():
        halo_ref[...] = jnp.zeros_like(halo_ref)

    u = u_ref[...]
    ext = jnp.concatenate([halo_ref[...], u], axis=0)
    pos = (i % tps) * tm + lax.broadcasted_iota(jnp.int32, (tm, LANES), 0)
    groups = []
    for g, w in enumerate(POOL_WINDOWS):
        e = ext[:, g * LANES:(g + 1) * LANES]
        s, span = e, 1
        while span < w:
            s = s + pltpu.roll(s, span, 0)
            span *= 2
        cnt = jnp.minimum(pos + 1, w).astype(F32)
        groups.append(s[POOL_HALO:] / cnt - e[POOL_HALO:])
    x1 = _mix_tail(x, groups, yb_ref[...], nm_ref, wg_ref, pw_ref, ps_ref, wpa_ref, wpb_ref, wout_ref)
    tail = u[tm - POOL_HALO:]
    halo_ref[...] = tail
    st_ref[0] = tail
    return x1


def _mix_sample_kernel(x_ref, st_ref, u_ref, yb_ref, nm_ref, wg_ref, pw_ref, ps_ref, wpa_ref, wpb_ref,
                       wout_ref, o_ref, *, bs, ts, past_len):
    slabs = [st_ref[j * bs:(j + 1) * bs, :] for j in range(POOL_STATE)]
    slabs += [u_ref[t * bs:(t + 1) * bs, :] for t in range(ts)]
    groups = []
    for g, w in enumerate(POOL_WINDOWS):
        e = [s[:, g * LANES:(g + 1) * LANES] for s in slabs]
        s, span = e, 1
        while span < w:
            s = [s[j] + s[j - span] if j >= span else s[j] for j in range(len(s))]
            span *= 2
        rows = []
        for t in range(ts):
            cnt = float(min(past_len + t + 1, w))
            rows.append(s[POOL_STATE + t] / cnt - e[POOL_STATE + t])
        groups.append(jnp.concatenate(rows, axis=0))
    o_ref[...] = _mix_tail(x_ref[...], groups, yb_ref[...], nm_ref, wg_ref,
                           pw_ref, ps_ref, wpa_ref, wpb_ref, wout_ref)


def _mix_sample(x_tm, state_tm, u, yb, norm_g, w_gates, pool_w_bf, pool_scale, wpa, wpb, wout, *,
                bs, ts, past_len):
    n, d = x_tm.shape
    pw = u.shape[1]
    args = (x_tm, state_tm, u, yb, norm_g.reshape(1, d), w_gates, pool_w_bf, pool_scale.reshape(1, pw),
            wpa, wpb, wout)
    return pl.pallas_call(
        functools.partial(_mix_sample_kernel, bs=bs, ts=ts, past_len=past_len),
        grid=(1,),
        in_specs=[_const_spec(a.shape) for a in args],
        out_specs=_const_spec((n, d)),
        out_shape=jax.ShapeDtypeStruct((n, d), F32),
        compiler_params=_cparams("arbitrary"),
        name="mix_sample",
    )(*args)


def _gelu_tanh(x):
    return 0.5 * x * (1.0 + jnp.tanh(0.7978845608028654 * (x + 0.044715 * (x * x * x))))


def _ffn_tail(x, a0, a1, a2, gate, cw_ref, cb_ref, wd_ref, nf_ref):
    conv = cb_ref[...] + a0 * cw_ref[0:1, :] + a1 * cw_ref[1:2, :] + a2 * cw_ref[2:3, :]
    hid = (_gelu_tanh(conv) * gate).astype(BF16)
    x2 = x + _dot(hid, wd_ref[...])
    return _rmsnorm(x2, nf_ref[...])


def _tail_prompt_kernel(x_ref, u_ref, yb_ref, nm_ref, wg_ref, pw_ref, ps_ref, wpa_ref, wpb_ref, wout_ref,
                        g_ref, wa_ref, wb_ref, cw_ref, cb_ref, wd_ref, nf_ref,
                        o_ref, pool_st_ref, st_ref, pool_halo_ref, halo_ref, *, tps, tm):
    i = pl.program_id(0)

    @pl.when((i % tps) == 0)
    def _():
        halo_ref[...] = jnp.zeros_like(halo_ref)

    x = _mix_prompt_rows(x_ref[...], u_ref, yb_ref, nm_ref, wg_ref, pw_ref, ps_ref, wpa_ref, wpb_ref,
                         wout_ref, pool_st_ref, pool_halo_ref, tps=tps, tm=tm)
    h = _rmsnorm(x, g_ref[...]).astype(BF16)
    a = _dot(h, wa_ref[...])
    gate = _dot(h, wb_ref[...])
    ext = jnp.concatenate([halo_ref[...], a], axis=0)
    a1 = pltpu.roll(ext, 1, 0)[CONV_HALO:]
    a0 = pltpu.roll(ext, 2, 0)[CONV_HALO:]
    o_ref[...] = _ffn_tail(x, a0, a1, a, gate, cw_ref, cb_ref, wd_ref, nf_ref)
    tail = a[tm - CONV_HALO:]
    halo_ref[...] = tail
    st_ref[0] = tail


def _tail_prompt(x2d, u, yb, norm_mix_g, w_gates, pool_w_bf, pool_scale, wpa, wpb, wout,
                 norm_ffn_g, wa, wb, conv_w, conv_b, wd, norm_final, *, tm, seq_len):
    n, d = x2d.shape
    pw = u.shape[1]
    aw = yb.shape[1]
    f = wa.shape[1]
    tps = seq_len // tm
    row = lambda w: pl.BlockSpec((tm, w), lambda i: (i, 0))
    seq = lambda r, w: pl.BlockSpec((1, r, w), lambda i: (i // tps, 0, 0))
    consts = (norm_mix_g.reshape(1, d), w_gates, pool_w_bf, pool_scale.reshape(1, pw), wpa, wpb, wout,
              norm_ffn_g.reshape(1, d), wa, wb, conv_w, conv_b.reshape(1, f), wd, norm_final.reshape(1, d))
    return pl.pallas_call(
        functools.partial(_tail_prompt_kernel, tps=tps, tm=tm),
        grid=(n // tm,),
        in_specs=[row(d), row(pw), row(aw)] + [_resident_spec(c.shape) for c in consts],
        out_specs=[row(d), seq(POOL_HALO, pw), seq(CONV_HALO, f)],
        out_shape=[jax.ShapeDtypeStruct((n, d), F32),
                   jax.ShapeDtypeStruct((n // seq_len, POOL_HALO, pw), F32),
                   jax.ShapeDtypeStruct((n // seq_len, CONV_HALO, f), F32)],
        scratch_shapes=[pltpu.VMEM((POOL_HALO, pw), F32), pltpu.VMEM((CONV_HALO, f), F32)],
        compiler_params=_cparams("arbitrary"),
        name="tail_prompt",
    )(x2d, u, yb, *consts)


def _ffn_sample_kernel(x_ref, st_ref, g_ref, wa_ref, wb_ref, cw_ref, cb_ref, wd_ref, nf_ref,
                       o_ref, a_ref, *, bs):
    x = x_ref[...]
    n = x.shape[0]
    h = _rmsnorm(x, g_ref[...]).astype(BF16)
    a = _dot(h, wa_ref[...])
    gate = _dot(h, wb_ref[...])
    ext = jnp.concatenate([st_ref[...], a], axis=0)
    a0 = ext[:n]
    a1 = ext[bs:bs + n]
    o_ref[...] = _ffn_tail(x, a0, a1, a, gate, cw_ref, cb_ref, wd_ref, nf_ref)
    a_ref[...] = ext[n:]


def _ffn_sample(x_tm, state_tm, norm_g, wa, wb, conv_w, conv_b, wd, norm_final, *, bs):
    n, d = x_tm.shape
    f = wa.shape[1]
    args = (x_tm, state_tm, norm_g.reshape(1, d), wa, wb, conv_w, conv_b.reshape(1, f), wd,
            norm_final.reshape(1, d))
    return pl.pallas_call(
        functools.partial(_ffn_sample_kernel, bs=bs),
        grid=(1,),
        in_specs=[_const_spec(a.shape) for a in args],
        out_specs=[_const_spec((n, d)), _const_spec((CONV_STATE * bs, f))],
        out_shape=[jax.ShapeDtypeStruct((n, d), F32), jax.ShapeDtypeStruct((CONV_STATE * bs, f), F32)],
        compiler_params=_cparams("arbitrary"),
        name="ffn_sample",
    )(*args)


def _tile(n, pref):
    t = min(pref, n)
    assert n % t == 0, (n, t)
    return t


def kernel(x_prompt, x_sample, cache_k, cache_v, page_table, state_pool, state_conv, norm_mix, w_in, sb_bias, pool_w, pool_scale, w_proj_a, w_proj_b, w_out, norm_ffn, w_up_a, w_up_b, conv_w, conv_b, w_down, norm_final):
    depth = norm_mix.shape[0]
    bp, t, d = x_prompt.shape
    bs, ts, _ = x_sample.shape
    n_heads, hd = cache_k.shape[-2:]
    assert hd == HEAD_DIM
    aw = n_heads * hd
    pw = pool_scale.shape[-1]
    n_phys, page = cache_k.shape[1:3]
    past_len = page_table.shape[1] * page
    assert pw == LANES * len(POOL_WINDOWS) and state_pool.shape[2] == POOL_STATE
    assert state_conv.shape[2] == CONV_STATE and conv_w.shape[1] == CONV_WIDTH

    tm = _tile(t, 512)
    tm_tail = _tile(t, 512)
    tk = _tile(t, 256)
    tq = _tile(t, 512)
    pps = _tile(page_table.shape[1], 32)

    def pages_keys_last(cache):
        return jnp.transpose(cache, (0, 2, 3, 1)).reshape(n_phys, aw, page)

    assert depth == 1, "the final norm is fused into the (single) layer's FFN kernel"
    l = 0
    kv_cols = pw + 3 * aw
    w_in_bf = w_in[l].astype(BF16)
    w_qkv, w_gates = w_in_bf[:, :kv_cols], w_in_bf[:, kv_cols:]
    pool_w_bf = pool_w[l].astype(BF16)
    wpa, wpb, wout = w_proj_a[l].astype(BF16), w_proj_b[l].astype(BF16), w_out[l].astype(BF16)
    wa, wb, wd = w_up_a[l].astype(BF16), w_up_b[l].astype(BF16), w_down[l].astype(BF16)

    hp = x_prompt.reshape(bp * t, d)
    u, q, kt, vt, kt_bf, v_bf = _in_proj(hp, norm_mix[l], w_qkv, tm=tm, seq_len=t, pw=pw, aw=aw,
                                         q_dtype=BF16)
    yb = _prompt_attention(q.reshape(bp, t, aw), kt_bf, v_bf.reshape(bp, t, aw), sb_bias[l],
                           tq=tq, tk=tk, groups=aw // LANES)
    y_prompt, pool_tail, conv_tail = _tail_prompt(
        hp, u, yb.reshape(bp * t, aw), norm_mix[l], w_gates, pool_w_bf, pool_scale[l], wpa, wpb, wout,
        norm_ffn[l], wa, wb, conv_w[l], conv_b[l], wd, norm_final, tm=tm_tail, seq_len=t)
    rows_major = lambda a: jnp.transpose(a.reshape(bp, n_heads, hd, t), (0, 3, 1, 2))
    k_prompt, v_prompt = rows_major(kt), rows_major(vt)
    pool_prompt = pool_tail[:, POOL_HALO - POOL_STATE:]
    conv_prompt = conv_tail[:, CONV_HALO - CONV_STATE:]

    hs = jnp.transpose(x_sample, (1, 0, 2)).reshape(ts * bs, d)
    us, qs, ksn, vsn = _in_proj(hs, norm_mix[l], w_qkv, tm=_tile(ts * bs, 256), seq_len=None,
                                pw=pw, aw=aw, q_dtype=F32)
    to_bm = lambda a: jnp.transpose(a.reshape(ts, bs, -1), (1, 0, 2))
    ks_bm, vs_bm = to_bm(ksn), to_bm(vsn)
    ybs = _decode_attention(to_bm(qs), ks_bm, vs_bm, pages_keys_last(cache_k[l]),
                            pages_keys_last(cache_v[l]), page_table, sb_bias[l], pages_per_step=pps)
    ybs_tm = jnp.transpose(ybs, (1, 0, 2)).reshape(ts * bs, aw)
    pool_tm = jnp.transpose(state_pool[l], (1, 0, 2)).reshape(POOL_STATE * bs, pw)
    hs = _mix_sample(hs, pool_tm, us, ybs_tm, norm_mix[l], w_gates, pool_w_bf, pool_scale[l],
                     wpa, wpb, wout, bs=bs, ts=ts, past_len=past_len)
    conv_tm = jnp.transpose(state_conv[l], (1, 0, 2)).reshape(CONV_STATE * bs, -1)
    hs_out, a_tail = _ffn_sample(hs, conv_tm, norm_ffn[l], wa, wb, conv_w[l], conv_b[l], wd,
                                 norm_final, bs=bs)
    y_sample = jnp.transpose(hs_out.reshape(ts, bs, d), (1, 0, 2))
    k_sample = ks_bm.reshape(bs, ts, n_heads, hd)
    v_sample = vs_bm.reshape(bs, ts, n_heads, hd)
    pool_sample = jnp.concatenate([state_pool[l], to_bm(us)], axis=1)[:, -POOL_STATE:]
    conv_sample = jnp.transpose(a_tail.reshape(CONV_STATE, bs, -1), (1, 0, 2))

    layer = lambda a: a[None]
    return (y_prompt.reshape(bp, t, d), y_sample, layer(k_prompt), layer(v_prompt),
            layer(pool_prompt), layer(conv_prompt), layer(k_sample), layer(v_sample),
            layer(pool_sample), layer(conv_sample))
```
